```python
import math
import jax
import jax.numpy as jnp
from jax import lax
import numpy as np

D_MODEL = 2048
BATCH = 4
SEQ = 2048
DEPTH = 4
DEC_BATCH = 128
DEC_SEQ = 8
PAST_LEN = 16384
PAGE_SIZE = 128

N_MIXERS = 2
N_A_LAYERS = (DEPTH + 1) // 2
N_B_LAYERS = DEPTH // 2
EPS = 1e-6

CHUNK_A = 128
A_GROUP_DIM = 128
A_WIDTH = 2 * D_MODEL
A_GROUPS = A_WIDTH // A_GROUP_DIM

DK = 128
DV = 128
N_QK_HEADS = D_MODEL // DK
N_V_HEADS = 2 * N_QK_HEADS
KEY_DIM = N_QK_HEADS * DK
VALUE_DIM = N_V_HEADS * DV
CONV_W = 4
CONV_DIM = 2 * KEY_DIM + VALUE_DIM
B_IN_DIM = CONV_DIM + VALUE_DIM + 2 * N_V_HEADS
CHUNK_B = 64

N_GROUPS = 8
EXPERTS_PER_GROUP = 8
N_EXPERTS = N_GROUPS * EXPERTS_PER_GROUP
TOP_K = 2
D_EXPERT = D_MODEL // 4
MOE_BLOCK = 128

kernel_name = "hybrid_chunkgmlp_gdn_hiermoe_step"


def rmsnorm(x, w):
    xf = x.astype(jnp.float32)
    y = xf * lax.rsqrt(jnp.mean(xf * xf, axis=-1, keepdims=True) + EPS)
    return (y * w.astype(jnp.float32)).astype(x.dtype)


def layernorm(x, g, b):
    xf = x.astype(jnp.float32)
    xc = xf - jnp.mean(xf, axis=-1, keepdims=True)
    var = jnp.mean(xc * xc, axis=-1, keepdims=True)
    y = xc * lax.rsqrt(var + EPS) * g.astype(jnp.float32) + b.astype(jnp.float32)
    return y.astype(x.dtype)


def l2norm(x):
    xf = x.astype(jnp.float32)
    return xf * lax.rsqrt(jnp.sum(xf * xf, axis=-1, keepdims=True) + EPS)


def chunk_gmlp_mixer(h, w_in, ln_g, ln_b, w_s, b_s, w_out):
    B, T, _ = h.shape
    z = jax.nn.gelu(h @ w_in)
    u, v = jnp.split(z, 2, axis=-1)
    v = layernorm(v, ln_g, ln_b)
    n_chunks = -(-T // CHUNK_A)
    pad = n_chunks * CHUNK_A - T
    vp = jnp.pad(v, ((0, 0), (0, pad), (0, 0))).reshape(B, n_chunks, CHUNK_A, A_GROUPS, A_GROUP_DIM)
    causal = jnp.tril(jnp.ones((CHUNK_A, CHUNK_A), dtype=bool))
    ws = jnp.where(causal, w_s, jnp.zeros_like(w_s))
    s = jnp.einsum('gts,bnsgd->bntgd', ws, vp) + b_s.T[None, None, :, :, None]
    s = s.reshape(B, n_chunks * CHUNK_A, A_WIDTH)[:, :T]
    y = (u * s) @ w_out
    return y, v


def chunk_gated_delta_rule(q, k, v, g, beta, S0):
    B, T, H, _ = q.shape
    nc = -(-T // CHUNK_B)
    pad = nc * CHUNK_B - T

    def to_chunks(t):
        t = jnp.pad(t.astype(jnp.float32), [(0, 0), (0, pad)] + [(0, 0)] * (t.ndim - 2))
        t = t.reshape((B, nc, CHUNK_B) + t.shape[2:])
        return jnp.moveaxis(t, (1, 3), (0, 2))

    qc, kc, vc = to_chunks(q), to_chunks(k), to_chunks(v)
    gc, bc = to_chunks(g), to_chunks(beta)
    gcum = jnp.cumsum(gc, axis=-1)
    incl = jnp.tril(jnp.ones((CHUNK_B, CHUNK_B), dtype=bool))
    strict = jnp.tril(jnp.ones((CHUNK_B, CHUNK_B), dtype=bool), k=-1)
    diff = gcum[..., :, None] - gcum[..., None, :]
    decay = jnp.exp(jnp.where(incl, diff, -jnp.inf))
    kk = jnp.einsum('nbhtk,nbhsk->nbhts', kc, kc)
    L = jnp.where(strict, kk * decay * bc[..., :, None], 0.0)
    eye = jnp.eye(CHUNK_B, dtype=jnp.float32)
    rhs = jnp.concatenate([vc * bc[..., None], kc * (bc * jnp.exp(gcum))[..., None]], axis=-1)
    sol = lax.linalg.triangular_solve(L + eye, rhs, left_side=True, lower=True, unit_diagonal=True)
    u_v, w_k = sol[..., :DV], sol[..., DV:]
    qk = jnp.einsum('nbhtk,nbhsk->nbhts', qc, kc) * decay
    q_g = qc * jnp.exp(gcum)[..., None]
    k_tail = kc * jnp.exp(gcum[..., -1:] - gcum)[..., None]
    g_tail = jnp.exp(gcum[..., -1])

    def step(S, xs):
        uv, wk, qkc, qg, kt, gt = xs
        U = uv - jnp.einsum('bhtk,bhkv->bhtv', wk, S)
        o = jnp.einsum('bhtk,bhkv->bhtv', qg, S) + jnp.einsum('bhts,bhsv->bhtv', qkc, U)
        S = S * gt[..., None, None] + jnp.einsum('bhtk,bhtv->bhkv', kt, U)
        return S, o

    S_final, o = lax.scan(step, S0.astype(jnp.float32), (u_v, w_k, qk, q_g, k_tail, g_tail))
    o = jnp.transpose(o, (1, 0, 3, 2, 4)).reshape(B, nc * CHUNK_B, H, DV)[:, :T]
    return o, S_final


def gated_deltanet_mixer(h, conv_state, S0, w_in, conv_w, a_log, dt_bias, norm_w, w_out):
    B, T, _ = h.shape
    proj = h @ w_in
    qkv, z, b, a = jnp.split(proj, [CONV_DIM, CONV_DIM + VALUE_DIM, CONV_DIM + VALUE_DIM + N_V_HEADS], axis=-1)
    xc = jnp.concatenate([conv_state.astype(qkv.dtype), qkv], axis=1)
    conv = xc[:, 0:T] * conv_w[0]
    for i in range(1, CONV_W):
        conv = conv + xc[:, i:i + T] * conv_w[i]
    qkv_c = jax.nn.silu(conv)
    new_conv = xc[:, T:]
    q, k, v = jnp.split(qkv_c, [KEY_DIM, 2 * KEY_DIM], axis=-1)
    q = l2norm(q.reshape(B, T, N_QK_HEADS, DK)) * (DK ** -0.5)
    k = l2norm(k.reshape(B, T, N_QK_HEADS, DK))
    v = v.reshape(B, T, N_V_HEADS, DV)
    rep = N_V_HEADS // N_QK_HEADS
    q = jnp.repeat(q, rep, axis=2)
    k = jnp.repeat(k, rep, axis=2)
    beta = jax.nn.sigmoid(b.astype(jnp.float32))
    g = -jnp.exp(a_log.astype(jnp.float32)) * jax.nn.softplus(a.astype(jnp.float32) + dt_bias.astype(jnp.float32))
    o, S_new = chunk_gated_delta_rule(q, k, v, g, beta, S0)
    o = rmsnorm(o, norm_w).astype(h.dtype) * jax.nn.silu(z.reshape(B, T, N_V_HEADS, DV))
    y = o.reshape(B, T, VALUE_DIM) @ w_out
    return y, S_new, new_conv


def hier_moe(h, w_group, b_group, w_router, b_router, w_gate, w_up, w_down):
    B, T, D = h.shape
    x = h.reshape(-1, D)
    N = x.shape[0]
    gp = jax.nn.softmax((x @ w_group).astype(jnp.float32) + b_group.astype(jnp.float32), axis=-1)
    g_sel = jnp.argmax(gp, axis=-1)
    p_group = jnp.max(gp, axis=-1, keepdims=True)
    el = ((x @ w_router).astype(jnp.float32) + b_router.astype(jnp.float32)).reshape(N, N_GROUPS, EXPERTS_PER_GROUP)
    el_sel = jnp.take_along_axis(el, g_sel[:, None, None], axis=1)[:, 0]
    ep = jax.nn.softmax(el_sel, axis=-1)
    top_p, top_i = lax.top_k(ep, TOP_K)
    gates = p_group * top_p / jnp.sum(top_p, axis=-1, keepdims=True)
    expert_ids = g_sel[:, None] * EXPERTS_PER_GROUP + top_i
    A = N * TOP_K
    e = expert_ids.reshape(-1)
    w = gates.reshape(-1)
    tok = jnp.repeat(jnp.arange(N), TOP_K)
    order = jnp.argsort(e)
    e_s, tok_s, w_s = e[order], tok[order], w[order]
    counts = jnp.bincount(e, length=N_EXPERTS)
    padded = (counts + MOE_BLOCK - 1) // MOE_BLOCK * MOE_BLOCK
    start = jnp.cumsum(counts) - counts
    pend = jnp.cumsum(padded)
    pstart = pend - padded
    dest = pstart[e_s] + jnp.arange(A) - start[e_s]
    n_blocks = -(-A // MOE_BLOCK) + N_EXPERTS
    xb = jnp.zeros((n_blocks * MOE_BLOCK, D), x.dtype).at[dest].set(x[tok_s])
    block_expert = jnp.minimum(jnp.searchsorted(pend, jnp.arange(n_blocks) * MOE_BLOCK, side='right'), N_EXPERTS - 1)

    def expert_block(args):
        xblk, eid = args
        hid = jax.nn.silu(xblk @ w_gate[eid]) * (xblk @ w_up[eid])
        return hid @ w_down[eid]

    yb = lax.map(expert_block, (xb.reshape(n_blocks, MOE_BLOCK, D), block_expert)).reshape(-1, D)
    y = jnp.zeros_like(x).at[tok_s].add(yb[dest] * w_s[:, None].astype(yb.dtype))
    return y.reshape(B, T, D)


def trunk(x, conv_states, S_states, norm_mixer_w, norm_ffn_w, norm_final_w,
          a_w_in, a_ln_g, a_ln_b, a_w_s, a_b_s, a_w_out,
          b_w_in, b_conv_w, b_a_log, b_dt_bias, b_norm_w, b_w_out,
          moe_w_group, moe_b_group, moe_w_router, moe_b_router, moe_w_gate, moe_w_up, moe_w_down):
    chunk_v, new_S, new_conv = [], [], []
    for i in range(DEPTH):
        h = rmsnorm(x, norm_mixer_w[i])
        j = i // N_MIXERS
        if i % N_MIXERS == 0:
            y, v = chunk_gmlp_mixer(h, a_w_in[j], a_ln_g[j], a_ln_b[j], a_w_s[j], a_b_s[j], a_w_out[j])
            chunk_v.append(v)
        else:
            y, S, cs = gated_deltanet_mixer(h, conv_states[j], S_states[j], b_w_in[j], b_conv_w[j],
                                            b_a_log[j], b_dt_bias[j], b_norm_w[j], b_w_out[j])
            new_S.append(S)
            new_conv.append(cs)
        x = x + y
        x = x + hier_moe(rmsnorm(x, norm_ffn_w[i]), moe_w_group[i], moe_b_group[i], moe_w_router[i],
                         moe_b_router[i], moe_w_gate[i], moe_w_up[i], moe_w_down[i])
    return rmsnorm(x, norm_final_w), jnp.stack(chunk_v), jnp.stack(new_S), jnp.stack(new_conv)


def setup_inputs(seed: int = 0) -> dict:
    key = jax.random.key(seed)
    ks = jax.random.split(key, 26)
    f32 = jnp.float32
    nrm = lambda k, s: jax.random.normal(k, s, f32)
    x_prompt = nrm(ks[0], (BATCH, SEQ, D_MODEL))
    x_sample = nrm(ks[1], (DEC_BATCH, DEC_SEQ, D_MODEL))
    state_delta_S = 0.05 * nrm(ks[2], (N_B_LAYERS, DEC_BATCH, N_V_HEADS, DK, DV))
    state_delta_conv = nrm(ks[3], (N_B_LAYERS, DEC_BATCH, CONV_W - 1, CONV_DIM))
    norm_mixer_w = 1.0 + 0.01 * nrm(ks[4], (DEPTH, D_MODEL))
    norm_ffn_w = 1.0 + 0.01 * nrm(ks[5], (DEPTH, D_MODEL))
    norm_final_w = 1.0 + 0.01 * nrm(ks[6], (D_MODEL,))
    a_w_in = nrm(ks[7], (N_A_LAYERS, D_MODEL, 2 * A_WIDTH)) * (D_MODEL ** -0.5)
    a_ln_g = 1.0 + 0.01 * nrm(ks[8], (N_A_LAYERS, A_WIDTH))
    a_ln_b = 0.01 * nrm(ks[9], (N_A_LAYERS, A_WIDTH))
    a_w_s = nrm(ks[10], (N_A_LAYERS, A_GROUPS, CHUNK_A, CHUNK_A)) * (CHUNK_A ** -0.5)
    a_b_s = 1.0 + 0.1 * nrm(ks[11], (N_A_LAYERS, A_GROUPS, CHUNK_A))
    a_w_out = nrm(ks[12], (N_A_LAYERS, A_WIDTH, D_MODEL)) * (A_WIDTH ** -0.5)
    b_w_in = nrm(ks[13], (N_B_LAYERS, D_MODEL, B_IN_DIM)) * (D_MODEL ** -0.5)
    b_conv_w = nrm(ks[14], (N_B_LAYERS, CONV_W, CONV_DIM)) * (CONV_W ** -0.5)
    b_a_log = jnp.log(jax.random.uniform(ks[15], (N_B_LAYERS, N_V_HEADS), f32, 1.0, 16.0))
    dt = jnp.exp(jax.random.uniform(ks[16], (N_B_LAYERS, N_V_HEADS), f32, math.log(1e-3), math.log(1e-1)))
    b_dt_bias = dt + jnp.log(-jnp.expm1(-dt))
    b_norm_w = 1.0 + 0.01 * nrm(ks[17], (N_B_LAYERS, DV))
    b_w_out = nrm(ks[18], (N_B_LAYERS, VALUE_DIM, D_MODEL)) * (VALUE_DIM ** -0.5)
    moe_w_group = nrm(ks[19], (DEPTH, D_MODEL, N_GROUPS)) * (D_MODEL ** -0.5)
    moe_b_group = 0.01 * nrm(ks[20], (DEPTH, N_GROUPS))
    moe_w_router = nrm(ks[21], (DEPTH, D_MODEL, N_EXPERTS)) * (D_MODEL ** -0.5)
    moe_b_router = 0.01 * nrm(ks[22], (DEPTH, N_EXPERTS))
    moe_w_gate = nrm(ks[23], (DEPTH, N_EXPERTS, D_MODEL, D_EXPERT)) * (D_MODEL ** -0.5)
    moe_w_up = nrm(ks[24], (DEPTH, N_EXPERTS, D_MODEL, D_EXPERT)) * (D_MODEL ** -0.5)
    moe_w_down = nrm(ks[25], (DEPTH, N_EXPERTS, D_EXPERT, D_MODEL)) * (D_EXPERT ** -0.5)
    return {"x_prompt": x_prompt, "x_sample": x_sample,
            "state_delta_S": state_delta_S, "state_delta_conv": state_delta_conv,
            "norm_mixer_w": norm_mixer_w, "norm_ffn_w": norm_ffn_w, "norm_final_w": norm_final_w,
            "a_w_in": a_w_in, "a_ln_g": a_ln_g, "a_ln_b": a_ln_b, "a_w_s": a_w_s, "a_b_s": a_b_s, "a_w_out": a_w_out,
            "b_w_in": b_w_in, "b_conv_w": b_conv_w, "b_a_log": b_a_log, "b_dt_bias": b_dt_bias,
            "b_norm_w": b_norm_w, "b_w_out": b_w_out,
            "moe_w_group": moe_w_group, "moe_b_group": moe_b_group, "moe_w_router": moe_w_router,
            "moe_b_router": moe_b_router, "moe_w_gate": moe_w_gate, "moe_w_up": moe_w_up, "moe_w_down": moe_w_down}


def reference(x_prompt, x_sample, state_delta_S, state_delta_conv, norm_mixer_w, norm_ffn_w, norm_final_w,
              a_w_in, a_ln_g, a_ln_b, a_w_s, a_b_s, a_w_out,
              b_w_in, b_conv_w, b_a_log, b_dt_bias, b_norm_w, b_w_out,
              moe_w_group, moe_b_group, moe_w_router, moe_b_router, moe_w_gate, moe_w_up, moe_w_down):
    weights = (norm_mixer_w, norm_ffn_w, norm_final_w,
               a_w_in, a_ln_g, a_ln_b, a_w_s, a_b_s, a_w_out,
               b_w_in, b_conv_w, b_a_log, b_dt_bias, b_norm_w, b_w_out,
               moe_w_group, moe_b_group, moe_w_router, moe_b_router, moe_w_gate, moe_w_up, moe_w_down)
    nb = x_prompt.shape[0]
    zero_S = jnp.zeros((N_B_LAYERS, nb, N_V_HEADS, DK, DV), jnp.float32)
    zero_conv = jnp.zeros((N_B_LAYERS, nb, CONV_W - 1, CONV_DIM), x_prompt.dtype)
    y_prompt, _, delta_S_prompt, delta_conv_prompt = trunk(x_prompt, zero_conv, zero_S, *weights)
    y_sample, chunk_v_sample, delta_S_sample, delta_conv_sample = trunk(x_sample, state_delta_conv, state_delta_S, *weights)
    return (y_prompt, y_sample, chunk_v_sample, delta_S_prompt, delta_conv_prompt, delta_S_sample, delta_conv_sample)
```

```python
import functools

import jax
import jax.numpy as jnp
from jax import lax
from jax.experimental import pallas as pl
from jax.experimental.pallas import tpu as pltpu

F32 = jnp.float32
BF16 = jnp.bfloat16
I32 = jnp.int32

EPS = 1e-6
LANES = 128
SUBLANES = 8
CHUNK_A = 128
A_GROUP_DIM = 128
DK = 128
DV = 128
CONV_W = 4
CHUNK_B = 64
N_GROUPS = 8
EXPERTS_PER_GROUP = 8
N_EXPERTS = N_GROUPS * EXPERTS_PER_GROUP
TOP_K = 2
MOE_ROWS = 256
MIB = 2 ** 20


def _params(semantics, vmem_mib):
    return pltpu.CompilerParams(dimension_semantics=semantics, vmem_limit_bytes=vmem_mib * MIB)


def _pick(n, pref):
    t = min(n, pref)
    while n % t:
        t -= SUBLANES
    return t


def _rms(x, w):
    return x * lax.rsqrt(jnp.mean(x * x, axis=-1, keepdims=True) + EPS) * w


def _silu(x):
    return x * jax.nn.sigmoid(x)


def _rmsnorm_kernel(x_ref, w_ref, o_ref):
    o_ref[...] = _rms(x_ref[...], w_ref[...]).astype(o_ref.dtype)


def rmsnorm(x, w, out_dtype):
    n, d = x.shape
    tm = _pick(n, 512)
    return pl.pallas_call(
        _rmsnorm_kernel,
        out_shape=jax.ShapeDtypeStruct((n, d), out_dtype),
        grid=(n // tm,),
        in_specs=[pl.BlockSpec((tm, d), lambda i: (i, 0)),
                  pl.BlockSpec((1, d), lambda i: (0, 0))],
        out_specs=pl.BlockSpec((tm, d), lambda i: (i, 0)),
        compiler_params=_params(("arbitrary",), 32),
        name="rmsnorm",
    )(x, w.reshape(1, d))


def _mm_kernel(*refs, act, has_res, n_extra):
    a_ref, w_ref = refs[:2]
    extra = refs[2:2 + n_extra]
    rest = refs[2 + n_extra:]
    if has_res:
        r_ref, o_ref, wb_ref = rest
    else:
        o_ref, wb_ref = rest

    @pl.when(pl.program_id(1) == 0)
    def _():
        wb_ref[...] = w_ref[...].astype(BF16)

    acc = jnp.dot(a_ref[...], wb_ref[...], preferred_element_type=F32)
    if act == "gelu":
        acc = jax.nn.gelu(acc)
    elif act == "gdn_gates":
        a_log_ref, dt_ref = extra
        h = acc.shape[1] // 2
        lane = lax.broadcasted_iota(I32, acc.shape, 1)
        g = -jnp.exp(a_log_ref[...]) * jax.nn.softplus(acc + dt_ref[...])
        acc = jnp.where(lane < h, jax.nn.sigmoid(acc), g)
    if has_res:
        acc = acc + r_ref[...]
    o_ref[...] = acc.astype(o_ref.dtype)


def matmul(a, w, layer, col0, ncols, out_dtype, *, act=None, res=None, extra=(), tm=1024, tn=512):
    m, k = a.shape
    tm = _pick(m, tm)
    tn = min(tn, ncols)
    assert ncols % tn == 0 and col0 % tn == 0
    c0 = col0 // tn
    in_specs = [pl.BlockSpec((tm, k), lambda j, i: (i, 0)),
                pl.BlockSpec((None, k, tn), lambda j, i: (layer, 0, c0 + j))]
    args = [a, w]
    for e in extra:
        in_specs.append(pl.BlockSpec((1, tn), lambda j, i: (0, j)))
        args.append(e)
    if res is not None:
        in_specs.append(pl.BlockSpec((tm, tn), lambda j, i: (i, j)))
        args.append(res)
    out_bytes = jnp.dtype(out_dtype).itemsize
    vmem = (2 * tm * k * 2 + 2 * k * tn * 4 + k * tn * 2 + 2 * tm * tn * out_bytes
            + (2 * tm * tn * 4 if res is not None else 0) + 3 * tm * tn * 4)
    return pl.pallas_call(
        functools.partial(_mm_kernel, act=act, has_res=res is not None, n_extra=len(extra)),
        out_shape=jax.ShapeDtypeStruct((m, ncols), out_dtype),
        grid=(ncols // tn, m // tm),
        in_specs=in_specs,
        out_specs=pl.BlockSpec((tm, tn), lambda j, i: (i, j)),
        scratch_shapes=[pltpu.VMEM((k, tn), BF16)],
        compiler_params=_params(("arbitrary", "arbitrary"), min(56, vmem // MIB + 8)),
        name="matmul_" + (act or "plain") + ("_res" if res is not None else ""),
    )(*args)


def _gmlp_gate_kernel(u_ref, v_ref, g_ref, b_ref, ws_ref, bias_ref, o_ref, vo_ref, *, n_groups, first_sample):
    v = v_ref[...]
    xc = v - jnp.mean(v, axis=-1, keepdims=True)
    var = jnp.mean(xc * xc, axis=-1, keepdims=True)
    vn = xc * lax.rsqrt(var + EPS) * g_ref[...] + b_ref[...]

    @pl.when(pl.program_id(0) >= first_sample)
    def _():
        vo_ref[...] = vn

    vb = vn.astype(BF16)
    for g in range(n_groups):
        sl = slice(g * A_GROUP_DIM, (g + 1) * A_GROUP_DIM)
        s = jnp.dot(ws_ref[g], vb[:, sl], preferred_element_type=F32) + bias_ref[:, sl]
        o_ref[:, sl] = (u_ref[:, sl].astype(F32) * s).astype(BF16)


def gmlp_gate(u, vpre, ln_g, ln_b, ws2, bias2, n_prompt_chunks):
    n, aw = u.shape
    n_chunks = n // CHUNK_A
    n_groups = aw // A_GROUP_DIM
    n_sample_rows = n - n_prompt_chunks * CHUNK_A
    which = lambda c: jnp.where(c >= n_prompt_chunks, 1, 0)
    return pl.pallas_call(
        functools.partial(_gmlp_gate_kernel, n_groups=n_groups, first_sample=n_prompt_chunks),
        out_shape=(jax.ShapeDtypeStruct((n, aw), BF16),
                   jax.ShapeDtypeStruct((n_sample_rows, aw), F32)),
        grid=(n_chunks,),
        in_specs=[pl.BlockSpec((CHUNK_A, aw), lambda c: (c, 0)),
                  pl.BlockSpec((CHUNK_A, aw), lambda c: (c, 0)),
                  pl.BlockSpec((1, aw), lambda c: (0, 0)),
                  pl.BlockSpec((1, aw), lambda c: (0, 0)),
                  pl.BlockSpec((None, n_groups, CHUNK_A, CHUNK_A), lambda c: (which(c), 0, 0, 0)),
                  pl.BlockSpec((None, CHUNK_A, aw), lambda c: (which(c), 0, 0))],
        out_specs=(pl.BlockSpec((CHUNK_A, aw), lambda c: (c, 0)),
                   pl.BlockSpec((CHUNK_A, aw), lambda c: (jnp.maximum(c - n_prompt_chunks, 0), 0))),
        compiler_params=_params(("arbitrary",), 40),
        name="gmlp_gate",
    )(u, vpre, ln_g.reshape(1, aw), ln_b.reshape(1, aw), ws2, bias2)


def _conv_epilogue(y, o_ref, rows, n_q_blocks, n_qk_blocks):
    c = pl.program_id(1)
    y = _silu(y)
    tc = y.shape[1]

    @pl.when(c < n_qk_blocks)
    def _():
        scale = jnp.where(c < n_q_blocks, DK ** -0.5, 1.0).astype(F32)
        for j in range(tc // DK):
            blk = y[:, j * DK:(j + 1) * DK]
            nrm = blk * lax.rsqrt(jnp.sum(blk * blk, axis=-1, keepdims=True) + EPS)
            o_ref[rows, j * DK:(j + 1) * DK] = (nrm * scale).astype(o_ref.dtype)

    @pl.when(c >= n_qk_blocks)
    def _():
        o_ref[rows, :] = y.astype(o_ref.dtype)


def _conv_prompt_kernel(x_ref, st_ref, w_ref, o_ref, *, sub, n_q_blocks, n_qk_blocks):
    t_len, tc = x_ref.shape
    w = w_ref[...]
    r8 = lax.broadcasted_iota(I32, (SUBLANES, tc), 0)

    def body(i, carry):
        r0 = pl.multiple_of(i * sub, sub)
        cur = x_ref[pl.ds(r0, sub), :]
        p0 = pl.multiple_of(jnp.maximum(r0 - SUBLANES, 0), SUBLANES)
        prev8 = jnp.where(i == 0, st_ref[...], x_ref[pl.ds(p0, SUBLANES), :])
        acc = cur * w[CONV_W - 1:CONV_W, :]
        for k in range(1, CONV_W):
            xs = pltpu.roll(cur, k, 0)
            top = jnp.where(r8 < k, pltpu.roll(prev8, k, 0), xs[:SUBLANES])
            if sub > SUBLANES:
                xs = jnp.concatenate([top, xs[SUBLANES:]], axis=0)
            else:
                xs = top
            acc = acc + xs * w[CONV_W - 1 - k:CONV_W - k, :]
        _conv_epilogue(acc, o_ref, pl.ds(r0, sub), n_q_blocks, n_qk_blocks)
        return carry

    lax.fori_loop(0, t_len // sub, body, 0)


def _conv_sample_kernel(x_ref, st_ref, w_ref, o_ref, *, seq_len, n_q_blocks, n_qk_blocks):
    rows, tc = x_ref.shape
    assert seq_len == SUBLANES
    w = w_ref[...]
    x = x_ref[...]
    st = st_ref[...]
    t = lax.broadcasted_iota(I32, (rows, tc), 0) & (seq_len - 1)
    acc = x * w[CONV_W - 1:CONV_W, :]
    for k in range(1, CONV_W):
        xs = pltpu.roll(x, k, 0)
        ss = pltpu.roll(st, rows - (seq_len - k), 0)
        acc = acc + jnp.where(t >= k, xs, ss) * w[CONV_W - 1 - k:CONV_W - k, :]
    _conv_epilogue(acc, o_ref, slice(None), n_q_blocks, n_qk_blocks)


def gdn_conv_prompt(qkv, st8, conv_w, layer, n_seq, seq_len, key_dim):
    c_dim = qkv.shape[1]
    tc = 256
    sub = _pick(seq_len, 256)
    nqb, nqkb = key_dim // tc, 2 * key_dim // tc
    return pl.pallas_call(
        functools.partial(_conv_prompt_kernel, sub=sub, n_q_blocks=nqb, n_qk_blocks=nqkb),
        out_shape=jax.ShapeDtypeStruct((n_seq * seq_len, c_dim), BF16),
        grid=(n_seq, c_dim // tc),
        in_specs=[pl.BlockSpec((seq_len, tc), lambda b, c: (b, c)),
                  pl.BlockSpec((SUBLANES, tc), lambda b, c: (b, c)),
                  pl.BlockSpec((None, CONV_W, tc), lambda b, c: (layer, 0, c))],
        out_specs=pl.BlockSpec((seq_len, tc), lambda b, c: (b, c)),
        compiler_params=_params(("arbitrary", "arbitrary"), 32),
        name="gdn_conv_prompt",
    )(qkv, st8, conv_w)


def gdn_conv_sample(qkv, row0, st8, conv_w, layer, n_rows, seq_len, key_dim):
    c_dim = qkv.shape[1]
    tc = 512
    tr = _pick(n_rows, 256)
    assert row0 % tr == 0
    rb0 = row0 // tr
    nqb, nqkb = key_dim // tc, 2 * key_dim // tc
    return pl.pallas_call(
        functools.partial(_conv_sample_kernel, seq_len=seq_len, n_q_blocks=nqb, n_qk_blocks=nqkb),
        out_shape=jax.ShapeDtypeStruct((n_rows, c_dim), BF16),
        grid=(n_rows // tr, c_dim // tc),
        in_specs=[pl.BlockSpec((tr, tc), lambda r, c: (rb0 + r, c)),
                  pl.BlockSpec((tr, tc), lambda r, c: (r, c)),
                  pl.BlockSpec((None, CONV_W, tc), lambda r, c: (layer, 0, c))],
        out_specs=pl.BlockSpec((tr, tc), lambda r, c: (r, c)),
        compiler_params=_params(("arbitrary", "arbitrary"), 32),
        name="gdn_conv_sample",
    )(qkv, st8, conv_w)


def _dot_nt(a, b):
    return lax.dot_general(a, b, (((1,), (1,)), ((), ())), preferred_element_type=F32)


def _dot_tn(a, b):
    return lax.dot_general(a, b, (((0,), (0,)), ((), ())), preferred_element_type=F32)


def _bdot(a, b):
    return jnp.dot(a.astype(BF16), b.astype(BF16), preferred_element_type=F32)


def _delta_kernel(q_ref, k_ref, v_ref, z_ref, gb_ref, gt_ref, nw_ref, s0_ref, o_ref, s_ref, *,
                  n_chunks, n_heads, heads_per_step, seqs_per_step):
    c_len = CHUNK_B
    hp = pl.program_id(1)

    @pl.when(pl.program_id(2) == 0)
    def _():
        s_ref[...] = s0_ref[...]

    ri = lax.broadcasted_iota(I32, (c_len, c_len), 0)
    ci = lax.broadcasted_iota(I32, (c_len, c_len), 1)
    incl = ci <= ri
    strict = ci < ri
    eye = (ci == ri).astype(F32)
    lane_gb = lax.broadcasted_iota(I32, (c_len, 2 * n_heads), 1)
    sub_gt = lax.broadcasted_iota(I32, (n_heads, c_len), 0)
    n_sq = c_len.bit_length() - 2
    nw = nw_ref[...]

    for c in range(n_chunks):
        sq = c // (n_chunks // seqs_per_step)
        rs = slice(c * c_len, (c + 1) * c_len)
        kc = k_ref[rs, :]
        qc = q_ref[rs, :]
        kk = _dot_nt(kc, kc)
        qkm = _dot_nt(qc, kc)
        kf = kc.astype(F32)
        qf = qc.astype(F32)
        gb = gb_ref[rs, :]
        gt = gt_ref[:, rs]
        for hh in range(heads_per_step):
            h = hp * heads_per_step + hh
            hs = slice(hh * DV, (hh + 1) * DV)
            beta = jnp.sum(jnp.where(lane_gb == h, gb, 0.0), axis=-1, keepdims=True)
            g_col = jnp.sum(jnp.where(lane_gb == h + n_heads, gb, 0.0), axis=-1, keepdims=True)
            g_row = jnp.sum(jnp.where(sub_gt == h, gt, 0.0), axis=0, keepdims=True)
            gcum_col = jnp.sum(jnp.where(incl, g_row, 0.0), axis=-1, keepdims=True)
            gcum_row = jnp.sum(jnp.where(ri <= ci, g_col, 0.0), axis=0, keepdims=True)
            decay = jnp.exp(jnp.where(incl, gcum_col - gcum_row, -jnp.inf))
            low = jnp.where(strict, kk * decay * beta, 0.0)
            inv = eye - low
            lp = low
            for _ in range(n_sq):
                lp = _bdot(lp, lp)
                inv = inv + _bdot(inv, lp)
            egc = jnp.exp(gcum_col)
            vf = v_ref[rs, hs].astype(F32)
            rhs = jnp.concatenate([vf * beta, kf * (beta * egc)], axis=1)
            sol = _bdot(inv, rhs)
            u_v = sol[:, :DV]
            w_k = sol[:, DV:]
            qk = qkm * decay
            q_g = qf * egc
            g_last = gcum_col[c_len - 1:c_len, :]
            k_tail = kf * jnp.exp(g_last - gcum_col)
            s_old = s_ref[sq, hh]
            ws = _bdot(jnp.concatenate([w_k, q_g], axis=0), s_old)
            u_new = u_v - ws[:c_len]
            o = ws[c_len:] + _bdot(qk, u_new)
            s_ref[sq, hh] = s_old * jnp.exp(g_last) + _dot_tn(k_tail.astype(BF16), u_new.astype(BF16))
            zf = z_ref[rs, hs].astype(F32)
            o_ref[rs, hs] = (_rms(o, nw) * _silu(zf)).astype(o_ref.dtype)


def gated_delta(qkvc, z, gb, gt, norm_w, s0, n_seq, seq_len, key_dim, row_blocks0=0):
    n_heads = s0.shape[1]
    n_qk = key_dim // DK
    hps = n_heads // n_qk
    if seq_len >= LANES:
        sps, tc = 1, _pick(seq_len, 256)
    else:
        sps, tc = LANES // seq_len, LANES
    assert n_seq % sps == 0
    nt = sps * seq_len // tc
    vw = hps * DV
    v_blk0 = 2 * key_dim // vw
    rb = lambda b, t: row_blocks0 + b * nt + t
    return pl.pallas_call(
        functools.partial(_delta_kernel, n_chunks=tc // CHUNK_B, n_heads=n_heads, heads_per_step=hps,
                          seqs_per_step=sps),
        out_shape=(jax.ShapeDtypeStruct((n_seq * seq_len, n_heads * DV), BF16),
                   jax.ShapeDtypeStruct(s0.shape, F32)),
        grid=(n_seq // sps, n_qk, nt),
        in_specs=[pl.BlockSpec((tc, DK), lambda b, h, t: (rb(b, t), h)),
                  pl.BlockSpec((tc, DK), lambda b, h, t: (rb(b, t), n_qk + h)),
                  pl.BlockSpec((tc, vw), lambda b, h, t: (rb(b, t), v_blk0 + h)),
                  pl.BlockSpec((tc, vw), lambda b, h, t: (rb(b, t), h)),
                  pl.BlockSpec((tc, 2 * n_heads), lambda b, h, t: (rb(b, t), 0)),
                  pl.BlockSpec((n_heads, tc), lambda b, h, t: (0, rb(b, t))),
                  pl.BlockSpec((1, DV), lambda b, h, t: (0, 0)),
                  pl.BlockSpec((sps, hps, DK, DV), lambda b, h, t: (b, h, 0, 0))],
        out_specs=(pl.BlockSpec((tc, vw), lambda b, h, t: (b * nt + t, h)),
                   pl.BlockSpec((sps, hps, DK, DV), lambda b, h, t: (b, h, 0, 0))),
        compiler_params=_params(("arbitrary", "arbitrary", "arbitrary"), 32),
        name="gated_delta",
    )(qkvc, qkvc, qkvc, z, gb, gt, norm_w.reshape(1, DV), s0)


def _router_kernel(x_ref, nw_ref, wr_ref, br_ref, h_ref, sel_ref, gate_ref, cnt_ref, cnt_sc):
    tm = x_ref.shape[0]

    @pl.when(pl.program_id(0) == 0)
    def _():
        cnt_sc[...] = jnp.zeros_like(cnt_sc)

    h = _rms(x_ref[...], nw_ref[...])
    h_ref[...] = h
    logits = jnp.dot(h, wr_ref[...], precision=lax.Precision.HIGHEST, preferred_element_type=F32) + br_ref[...]
    lane = lax.broadcasted_iota(I32, (tm, LANES), 1)
    lane_f = lane.astype(F32)
    big = float(LANES)

    is_g = lane < N_GROUPS
    gl = jnp.where(is_g, logits, -jnp.inf)
    ge = jnp.exp(gl - jnp.max(gl, axis=-1, keepdims=True))
    gp = ge / jnp.sum(ge, axis=-1, keepdims=True)
    p_group = jnp.max(gp, axis=-1, keepdims=True)
    g_sel = jnp.min(jnp.where(is_g & (gp == p_group), lane_f, big), axis=-1, keepdims=True)

    e_lane = lane - N_GROUPS
    in_grp = (e_lane >= 0) & (e_lane < N_EXPERTS) & ((e_lane >> 3).astype(F32) == g_sel)
    el = jnp.where(in_grp, logits, -jnp.inf)
    ee = jnp.exp(el - jnp.max(el, axis=-1, keepdims=True))
    ep = ee / jnp.sum(ee, axis=-1, keepdims=True)
    p1 = jnp.max(ep, axis=-1, keepdims=True)
    i1 = jnp.min(jnp.where(in_grp & (ep == p1), lane_f, big), axis=-1, keepdims=True)
    rest = in_grp & (lane_f != i1)
    p2 = jnp.max(jnp.where(rest, ep, -1.0), axis=-1, keepdims=True)
    i2 = jnp.min(jnp.where(rest & (ep == p2), lane_f, big), axis=-1, keepdims=True)
    denom = p1 + p2
    gate1 = p_group * p1 / denom
    gate2 = p_group * p2 / denom
    e1 = i1 - float(N_GROUPS)
    e2 = i2 - float(N_GROUPS)

    onehot = (lane_f == e1) | (lane_f == e2)
    oh = jnp.where(onehot, 1.0, 0.0)
    rr = lax.broadcasted_iota(I32, (tm, tm), 0)
    cc = lax.broadcasted_iota(I32, (tm, tm), 1)
    tri = jnp.where(cc < rr, 1.0, 0.0).astype(BF16)
    before = jnp.dot(tri, oh.astype(BF16), preferred_element_type=F32) + cnt_sc[...]
    r1 = jnp.sum(jnp.where(lane_f == e1, before, 0.0), axis=-1, keepdims=True)
    r2 = jnp.sum(jnp.where(lane_f == e2, before, 0.0), axis=-1, keepdims=True)
    cnt_sc[...] = cnt_sc[...] + jnp.sum(oh, axis=0, keepdims=True)
    cnt_ref[...] = cnt_sc[...]

    sel = jnp.where(lane == 0, e1, jnp.where(lane == 1, e2, jnp.where(lane == 2, r1, jnp.where(lane == 3, r2, 0.0))))
    sel_ref[...] = sel.astype(I32)
    gate_ref[...] = jnp.where(lane == 0, gate1, jnp.where(lane == 1, gate2, 0.0))


def moe_router(x, norm_w, w_route, b_route):
    n, d = x.shape
    tm = _pick(n, 256)
    return pl.pallas_call(
        _router_kernel,
        out_shape=(jax.ShapeDtypeStruct((n, d), F32),
                   jax.ShapeDtypeStruct((n, LANES), I32),
                   jax.ShapeDtypeStruct((n, LANES), F32),
                   jax.ShapeDtypeStruct((1, LANES), F32)),
        grid=(n // tm,),
        in_specs=[pl.BlockSpec((tm, d), lambda i: (i, 0)),
                  pl.BlockSpec((1, d), lambda i: (0, 0)),
                  pl.BlockSpec((d, LANES), lambda i: (0, 0)),
                  pl.BlockSpec((1, LANES), lambda i: (0, 0))],
        out_specs=(pl.BlockSpec((tm, d), lambda i: (i, 0)),
                   pl.BlockSpec((tm, LANES), lambda i: (i, 0)),
                   pl.BlockSpec((tm, LANES), lambda i: (i, 0)),
                   pl.BlockSpec((1, LANES), lambda i: (0, 0))),
        scratch_shapes=[pltpu.VMEM((1, LANES), F32)],
        compiler_params=_params(("arbitrary",), 32),
        name="moe_router",
    )(x, norm_w.reshape(1, d), w_route, b_route)


def _row_copy_out(h_ref, xs_ref, sem, r, d):
    return pltpu.make_async_copy(h_ref.at[pl.ds(r, 1), :], xs_ref.at[pl.ds(d, 1), :], sem)


def _dispatch_kernel(dest_ref, h_ref, xs_in_ref, xs_ref, sem):
    del xs_in_ref
    tm = h_ref.shape[0]

    def issue(r, carry):
        for k in range(TOP_K):
            _row_copy_out(h_ref, xs_ref, sem, r, dest_ref[0, 0, TOP_K * r + k]).start()
        return carry

    lax.fori_loop(0, tm, issue, 0)

    def drain(r, carry):
        for k in range(TOP_K):
            _row_copy_out(h_ref, xs_ref, sem, r, dest_ref[0, 0, TOP_K * r + k]).wait()
        return carry

    lax.fori_loop(0, tm, drain, 0)


def moe_dispatch(h, dest, n_slots):
    n, d = h.shape
    tm = _pick(n, 256)
    dest3 = dest.reshape(n // tm, 1, TOP_K * tm)
    xs0 = jnp.zeros((n_slots, d), F32)
    return pl.pallas_call(
        _dispatch_kernel,
        out_shape=jax.ShapeDtypeStruct((n_slots, d), F32),
        grid=(n // tm,),
        in_specs=[pl.BlockSpec((1, 1, TOP_K * tm), lambda i: (i, 0, 0), memory_space=pltpu.SMEM),
                  pl.BlockSpec((tm, d), lambda i: (i, 0)),
                  pl.BlockSpec(memory_space=pl.ANY)],
        out_specs=pl.BlockSpec(memory_space=pl.ANY),
        scratch_shapes=[pltpu.SemaphoreType.DMA(())],
        input_output_aliases={2: 0},
        compiler_params=_params(("arbitrary",), 32),
        name="moe_dispatch",
    )(dest3, h, xs0)


def _ffn_kernel(be_ref, first_ref, nused_ref, x_ref, wg_ref, wu_ref, wd_ref, o_ref, wgb, wub, wdb):
    b = pl.program_id(0)
    valid = b < nused_ref[0]

    @pl.when(valid & (first_ref[b] == 1))
    def _():
        wgb[...] = wg_ref[...].astype(BF16)
        wub[...] = wu_ref[...].astype(BF16)
        wdb[...] = wd_ref[...].astype(BF16)

    @pl.when(valid)
    def _():
        x = x_ref[...].astype(BF16)
        gate = jnp.dot(x, wgb[...], preferred_element_type=F32)
        up = jnp.dot(x, wub[...], preferred_element_type=F32)
        hid = (_silu(gate) * up).astype(BF16)
        o_ref[...] = jnp.dot(hid, wdb[...], preferred_element_type=F32)

    @pl.when(jnp.logical_not(valid))
    def _():
        o_ref[...] = jnp.zeros_like(o_ref)


def moe_experts(xs, block_expert, block_first, n_used, w_gate, w_up, w_down, layer):
    n_slots, d = xs.shape
    de = w_gate.shape[-1]
    n_blocks = n_slots // MOE_ROWS
    xmap = lambda b, be, fi, nu: (jnp.minimum(b, nu[0] - 1), 0)
    grid_spec = pltpu.PrefetchScalarGridSpec(
        num_scalar_prefetch=3,
        grid=(n_blocks,),
        in_specs=[pl.BlockSpec((MOE_ROWS, d), xmap),
                  pl.BlockSpec((None, None, d, de), lambda b, be, fi, nu: (layer, be[b], 0, 0)),
                  pl.BlockSpec((None, None, d, de), lambda b, be, fi, nu: (layer, be[b], 0, 0)),
                  pl.BlockSpec((None, None, de, d), lambda b, be, fi, nu: (layer, be[b], 0, 0))],
        out_specs=pl.BlockSpec((MOE_ROWS, d), lambda b, be, fi, nu: (b, 0)),
        scratch_shapes=[pltpu.VMEM((d, de), BF16), pltpu.VMEM((d, de), BF16), pltpu.VMEM((de, d), BF16)],
    )
    return pl.pallas_call(
        _ffn_kernel,
        out_shape=jax.ShapeDtypeStruct((n_slots, d), F32),
        grid_spec=grid_spec,
        compiler_params=_params(("arbitrary",), 48),
        name="moe_experts",
    )(block_expert, block_first, n_used, xs, w_gate, w_up, w_down)


def _row_copy_in(yb_ref, buf_ref, sem, k, r, d):
    return pltpu.make_async_copy(yb_ref.at[pl.ds(d, 1), :], buf_ref.at[k, pl.ds(r, 1), :], sem)


def _combine_kernel(dest_ref, x_ref, gate_ref, nw_ref, yb_ref, xo_ref, ho_ref, buf_ref, sem):
    tm = x_ref.shape[0]

    def issue(r, carry):
        for k in range(TOP_K):
            _row_copy_in(yb_ref, buf_ref, sem, k, r, dest_ref[0, 0, TOP_K * r + k]).start()
        return carry

    lax.fori_loop(0, tm, issue, 0)

    def drain(r, carry):
        for k in range(TOP_K):
            _row_copy_in(yb_ref, buf_ref, sem, k, r, dest_ref[0, 0, TOP_K * r + k]).wait()
        return carry

    lax.fori_loop(0, tm, drain, 0)

    g = gate_ref[...]
    y = buf_ref[0] * g[:, 0:1] + buf_ref[1] * g[:, 1:2]
    xn = x_ref[...] + y
    xo_ref[...] = xn
    ho_ref[...] = _rms(xn, nw_ref[...]).astype(ho_ref.dtype)


def moe_combine(x, yb, dest, gates, next_norm_w, h_dtype):
    n, d = x.shape
    tm = _pick(n, 256)
    dest3 = dest.reshape(n // tm, 1, TOP_K * tm)
    return pl.pallas_call(
        _combine_kernel,
        out_shape=(jax.ShapeDtypeStruct((n, d), F32), jax.ShapeDtypeStruct((n, d), h_dtype)),
        grid=(n // tm,),
        in_specs=[pl.BlockSpec((1, 1, TOP_K * tm), lambda i: (i, 0, 0), memory_space=pltpu.SMEM),
                  pl.BlockSpec((tm, d), lambda i: (i, 0)),
                  pl.BlockSpec((tm, LANES), lambda i: (i, 0)),
                  pl.BlockSpec((1, d), lambda i: (0, 0)),
                  pl.BlockSpec(memory_space=pl.ANY)],
        out_specs=(pl.BlockSpec((tm, d), lambda i: (i, 0)),
                   pl.BlockSpec((tm, d), lambda i: (i, 0))),
        scratch_shapes=[pltpu.VMEM((TOP_K, tm, d), F32), pltpu.SemaphoreType.DMA(())],
        compiler_params=_params(("arbitrary",), 40),
        name="moe_combine",
    )(dest3, x, gates, next_norm_w.reshape(1, d), yb)


def hier_moe(x, ffn_norm_w, w_route, b_route, w_gate, w_up, w_down, layer, next_norm_w, h_dtype):
    n, _ = x.shape
    h, sel, gates, cnt = moe_router(x, ffn_norm_w, w_route, b_route)
    counts = cnt[0, :N_EXPERTS].astype(I32)
    padded = (counts + MOE_ROWS - 1) // MOE_ROWS * MOE_ROWS
    pend = jnp.cumsum(padded)
    pstart = pend - padded
    dest = pstart[sel[:, 0:TOP_K]] + sel[:, TOP_K:2 * TOP_K]
    n_blocks = -(-(n * TOP_K) // MOE_ROWS) + N_EXPERTS
    n_used = pend[-1] // MOE_ROWS
    blk = jnp.arange(n_blocks, dtype=I32)
    last = jnp.maximum(n_used - 1, 0)
    bstart = jnp.minimum(blk, last) * MOE_ROWS
    block_expert = jnp.minimum(jnp.searchsorted(pend, bstart, side="right"), N_EXPERTS - 1).astype(I32)
    block_first = (bstart == pstart[block_expert]).astype(I32)
    xs = moe_dispatch(h, dest, n_blocks * MOE_ROWS)
    yb = moe_experts(xs, block_expert, block_first, n_used.reshape(1).astype(I32), w_gate, w_up, w_down, layer)
    return moe_combine(x, yb, dest, gates, next_norm_w, h_dtype)


def kernel(x_prompt, x_sample, state_delta_S, state_delta_conv, norm_mixer_w, norm_ffn_w, norm_final_w,
           a_w_in, a_ln_g, a_ln_b, a_w_s, a_b_s, a_w_out,
           b_w_in, b_conv_w, b_a_log, b_dt_bias, b_norm_w, b_w_out,
           moe_w_group, moe_b_group, moe_w_router, moe_b_router, moe_w_gate, moe_w_up, moe_w_down):
    nb, seq, d = x_prompt.shape
    db, dseq, _ = x_sample.shape
    depth = norm_mixer_w.shape[0]
    n_p, n_s = nb * seq, db * dseq
    n = n_p + n_s
    a_width = a_ln_g.shape[1]
    n_groups_a = a_width // A_GROUP_DIM
    n_heads = b_a_log.shape[1]
    value_dim = n_heads * DV
    conv_dim = b_conv_w.shape[2]
    key_dim = (conv_dim - value_dim) // 2
    assert seq % CHUNK_A == 0 and n_s % CHUNK_A == 0 and CHUNK_A % dseq == 0 and dseq == SUBLANES
    assert seq % CHUNK_B == 0 and dseq <= CHUNK_B and dseq >= CONV_W - 1

    x = jnp.concatenate([x_prompt.reshape(n_p, d), x_sample.reshape(n_s, d)], axis=0)

    causal = jnp.tril(jnp.ones((CHUNK_A, CHUNK_A), bool))
    ws_p = jnp.where(causal, a_w_s, 0.0)
    per = CHUNK_A // dseq
    ws_s = jnp.where(causal[:dseq, :dseq], a_w_s[:, :, :dseq, :dseq], 0.0)
    ws_s = jnp.einsum("ij,lgts->lgitjs", jnp.eye(per, dtype=F32), ws_s).reshape(a_w_s.shape)
    ws2 = jnp.stack([ws_p, ws_s], axis=1).astype(BF16)
    bias_p = jnp.repeat(jnp.swapaxes(a_b_s, 1, 2), A_GROUP_DIM, axis=2)
    bias_s = jnp.tile(bias_p[:, :dseq], (1, per, 1))
    bias2 = jnp.stack([bias_p, bias_s], axis=1)

    w_route = jnp.concatenate([moe_w_group, moe_w_router,
                               jnp.zeros((depth, d, LANES - N_GROUPS - N_EXPERTS), F32)], axis=2)
    b_route = jnp.concatenate([moe_b_group, moe_b_router,
                               jnp.zeros((depth, LANES - N_GROUPS - N_EXPERTS), F32)], axis=1)
    w_ba = b_w_in[:, :, conv_dim + value_dim:]
    zeros_h = jnp.zeros_like(b_a_log)
    a_log2 = jnp.concatenate([zeros_h, b_a_log], axis=1)
    dt2 = jnp.concatenate([zeros_h, b_dt_bias], axis=1)

    chunk_v, s_prompt, s_sample, conv_prompt, conv_sample = [], [], [], [], []
    h = rmsnorm(x, norm_mixer_w[0], BF16)
    for i in range(depth):
        j = i // 2
        if i % 2 == 0:
            u = matmul(h, a_w_in, j, 0, a_width, BF16, act="gelu")
            vpre = matmul(h, a_w_in, j, a_width, a_width, F32, act="gelu")
            gated, v_s = gmlp_gate(u, vpre, a_ln_g[j], a_ln_b[j], ws2[j], bias2[j], n_p // CHUNK_A)
            chunk_v.append(v_s.reshape(db, dseq, a_width))
            x = matmul(gated, a_w_out, j, 0, d, F32, res=x)
        else:
            qkv = matmul(h, b_w_in, j, 0, conv_dim, F32)
            z = matmul(h, b_w_in, j, conv_dim, value_dim, BF16)
            gb = matmul(h, w_ba, j, 0, 2 * n_heads, F32, act="gdn_gates",
                        extra=(a_log2[j:j + 1], dt2[j:j + 1]))
            gt = gb[:, n_heads:].T
            qkv_p = qkv[:n_p].reshape(nb, seq, conv_dim)
            qkv_s = qkv[n_p:].reshape(db, dseq, conv_dim)
            conv_prompt.append(qkv_p[:, seq - (CONV_W - 1):])
            conv_sample.append(qkv_s[:, dseq - (CONV_W - 1):])
            pad8 = ((0, 0), (SUBLANES - (CONV_W - 1), 0), (0, 0))
            st8_p = jnp.zeros((nb * SUBLANES, conv_dim), F32)
            st8_s = jnp.pad(state_delta_conv[j], pad8).reshape(db * SUBLANES, conv_dim)
            qc_p = gdn_conv_prompt(qkv, st8_p, b_conv_w, j, nb, seq, key_dim)
            qc_s = gdn_conv_sample(qkv, n_p, st8_s, b_conv_w, j, n_s, dseq, key_dim)
            o_p, s_p = gated_delta(qc_p, z, gb, gt, b_norm_w[j],
                                   jnp.zeros((nb, n_heads, DK, DV), F32), nb, seq, key_dim)
            padt = lambda t: jnp.pad(t.reshape(db, dseq, -1), ((0, 0), (0, CHUNK_B - dseq), (0, 0))
                                     ).reshape(db * CHUNK_B, -1)
            gb_s = padt(gb[n_p:])
            o_s, s_s = gated_delta(padt(qc_s), padt(z[n_p:]), gb_s, gb_s[:, n_heads:].T, b_norm_w[j],
                                   state_delta_S[j], db, CHUNK_B, key_dim)
            s_prompt.append(s_p)
            s_sample.append(s_s)
            o_s = o_s.reshape(db, CHUNK_B, value_dim)[:, :dseq].reshape(n_s, value_dim)
            gated = jnp.concatenate([o_p, o_s], axis=0)
            x = matmul(gated, b_w_out, j, 0, d, F32, res=x)
        last = i == depth - 1
        next_w = norm_final_w if last else norm_mixer_w[i + 1]
        x, h = hier_moe(x, norm_ffn_w[i], w_route[i], b_route[i:i + 1], moe_w_gate, moe_w_up, moe_w_down, i,
                        next_w, F32 if last else BF16)

    y_prompt = h[:n_p].reshape(nb, seq, d)
    y_sample = h[n_p:].reshape(db, dseq, d)
    return (y_prompt, y_sample, jnp.stack(chunk_v), jnp.stack(s_prompt), jnp.stack(conv_prompt),
            jnp.stack(s_sample), jnp.stack(conv_sample))
```

```python
import functools

import jax
import jax.numpy as jnp
from jax import lax
from jax.experimental import pallas as pl
from jax.experimental.pallas import tpu as pltpu

F32 = jnp.float32
BF16 = jnp.bfloat16
I32 = jnp.int32

EPS = 1e-6
LANES = 128
SUBLANES = 8
CHUNK_A = 128
A_GROUP_DIM = 128
DK = 128
DV = 128
CONV_W = 4
CHUNK_B = 64
N_GROUPS = 8
EXPERTS_PER_GROUP = 8
N_EXPERTS = N_GROUPS * EXPERTS_PER_GROUP
TOP_K = 2
MOE_ROWS = 256
MIB = 2 ** 20


def _params(semantics, vmem_mib):
    return pltpu.CompilerParams(dimension_semantics=semantics, vmem_limit_bytes=vmem_mib * MIB)


def _pick(n, pref):
    t = min(n, pref)
    while n % t:
        t -= SUBLANES
    return t


def _rms(x, w):
    return x * lax.rsqrt(jnp.mean(x * x, axis=-1, keepdims=True) + EPS) * w


def _silu(x):
    return x * jax.nn.sigmoid(x)


def _rmsnorm_kernel(x_ref, w_ref, o_ref):
    o_ref[...] = _rms(x_ref[...], w_ref[...]).astype(o_ref.dtype)


def rmsnorm(x, w, out_dtype):
    n, d = x.shape
    tm = _pick(n, 512)
    return pl.pallas_call(
        _rmsnorm_kernel,
        out_shape=jax.ShapeDtypeStruct((n, d), out_dtype),
        grid=(n // tm,),
        in_specs=[pl.BlockSpec((tm, d), lambda i: (i, 0)),
                  pl.BlockSpec((1, d), lambda i: (0, 0))],
        out_specs=pl.BlockSpec((tm, d), lambda i: (i, 0)),
        compiler_params=_params(("arbitrary",), 32),
        name="rmsnorm",
    )(x, w.reshape(1, d))


def _mm_kernel(*refs, act, has_res, n_extra, w_t):
    a_ref, w_ref = refs[:2]
    extra = refs[2:2 + n_extra]
    rest = refs[2 + n_extra:]
    if has_res:
        r_ref, o_ref, wb_ref = rest
    else:
        o_ref, wb_ref = rest

    @pl.when(pl.program_id(1) == 0)
    def _():
        wb_ref[...] = w_ref[...].astype(BF16)

    if w_t:
        acc = _dot_nt(a_ref[...], wb_ref[...])
    else:
        acc = jnp.dot(a_ref[...], wb_ref[...], preferred_element_type=F32)
    if act == "gelu":
        acc = jax.nn.gelu(acc)
    elif act == "gdn_gates":
        a_log_ref, dt_ref = extra
        h = acc.shape[1] // 2
        lane = lax.broadcasted_iota(I32, acc.shape, 1)
        g = -jnp.exp(a_log_ref[...]) * jax.nn.softplus(acc + dt_ref[...])
        acc = jnp.where(lane < h, jax.nn.sigmoid(acc), g)
    if has_res:
        acc = acc + r_ref[...]
    o_ref[...] = acc.astype(o_ref.dtype)


def matmul(a, w, layer, col0, ncols, out_dtype, *, act=None, res=None, extra=(), tm=1024, tn=512, w_t=False):
    m, k = a.shape
    tm = _pick(m, tm)
    tn = min(tn, ncols)
    assert ncols % tn == 0 and col0 % tn == 0
    c0 = col0 // tn
    if w_t:
        w_spec = pl.BlockSpec((None, tn, k), lambda j, i: (layer, c0 + j, 0))
    else:
        w_spec = pl.BlockSpec((None, k, tn), lambda j, i: (layer, 0, c0 + j))
    in_specs = [pl.BlockSpec((tm, k), lambda j, i: (i, 0)), w_spec]
    args = [a, w]
    for e in extra:
        in_specs.append(pl.BlockSpec((1, tn), lambda j, i: (0, j)))
        args.append(e)
    if res is not None:
        in_specs.append(pl.BlockSpec((tm, tn), lambda j, i: (i, j)))
        args.append(res)
    out_bytes = jnp.dtype(out_dtype).itemsize
    vmem = (2 * tm * k * 2 + 2 * k * tn * 4 + k * tn * 2 + 2 * tm * tn * out_bytes
            + (2 * tm * tn * 4 if res is not None else 0) + 3 * tm * tn * 4)
    return pl.pallas_call(
        functools.partial(_mm_kernel, act=act, has_res=res is not None, n_extra=len(extra), w_t=w_t),
        out_shape=jax.ShapeDtypeStruct((m, ncols), out_dtype),
        grid=(ncols // tn, m // tm),
        in_specs=in_specs,
        out_specs=pl.BlockSpec((tm, tn), lambda j, i: (i, j)),
        scratch_shapes=[pltpu.VMEM((tn, k) if w_t else (k, tn), BF16)],
        compiler_params=_params(("arbitrary", "arbitrary"), min(56, vmem // MIB + 8)),
        name="matmul_" + (act or "plain") + ("_res" if res is not None else ""),
    )(*args)


def _gmlp_gate_kernel(u_ref, v_ref, g_ref, b_ref, ws_ref, bias_ref, o_ref, vo_ref, *, n_groups, first_sample):
    v = v_ref[...]
    xc = v - jnp.mean(v, axis=-1, keepdims=True)
    var = jnp.mean(xc * xc, axis=-1, keepdims=True)
    vn = xc * lax.rsqrt(var + EPS) * g_ref[...] + b_ref[...]

    @pl.when(pl.program_id(0) >= first_sample)
    def _():
        vo_ref[...] = vn

    vb = vn.astype(BF16)
    for g in range(n_groups):
        sl = slice(g * A_GROUP_DIM, (g + 1) * A_GROUP_DIM)
        s = jnp.dot(ws_ref[g], vb[:, sl], preferred_element_type=F32) + bias_ref[:, sl]
        o_ref[:, sl] = (u_ref[:, sl].astype(F32) * s).astype(BF16)


def gmlp_gate(u, vpre, ln_g, ln_b, ws2, bias2, n_prompt_chunks):
    n, aw = u.shape
    n_chunks = n // CHUNK_A
    n_groups = aw // A_GROUP_DIM
    n_sample_rows = n - n_prompt_chunks * CHUNK_A
    which = lambda c: jnp.where(c >= n_prompt_chunks, 1, 0)
    return pl.pallas_call(
        functools.partial(_gmlp_gate_kernel, n_groups=n_groups, first_sample=n_prompt_chunks),
        out_shape=(jax.ShapeDtypeStruct((n, aw), BF16),
                   jax.ShapeDtypeStruct((n_sample_rows, aw), F32)),
        grid=(n_chunks,),
        in_specs=[pl.BlockSpec((CHUNK_A, aw), lambda c: (c, 0)),
                  pl.BlockSpec((CHUNK_A, aw), lambda c: (c, 0)),
                  pl.BlockSpec((1, aw), lambda c: (0, 0)),
                  pl.BlockSpec((1, aw), lambda c: (0, 0)),
                  pl.BlockSpec((None, n_groups, CHUNK_A, CHUNK_A), lambda c: (which(c), 0, 0, 0)),
                  pl.BlockSpec((None, CHUNK_A, aw), lambda c: (which(c), 0, 0))],
        out_specs=(pl.BlockSpec((CHUNK_A, aw), lambda c: (c, 0)),
                   pl.BlockSpec((CHUNK_A, aw), lambda c: (jnp.maximum(c - n_prompt_chunks, 0), 0))),
        compiler_params=_params(("arbitrary",), 40),
        name="gmlp_gate",
    )(u, vpre, ln_g.reshape(1, aw), ln_b.reshape(1, aw), ws2, bias2)


def _conv_epilogue(y, o_ref, rows, n_q_blocks, n_qk_blocks):
    c = pl.program_id(1)
    y = _silu(y)
    tc = y.shape[1]

    @pl.when(c < n_qk_blocks)
    def _():
        scale = jnp.where(c < n_q_blocks, DK ** -0.5, 1.0).astype(F32)
        for j in range(tc // DK):
            blk = y[:, j * DK:(j + 1) * DK]
            nrm = blk * lax.rsqrt(jnp.sum(blk * blk, axis=-1, keepdims=True) + EPS)
            o_ref[rows, j * DK:(j + 1) * DK] = (nrm * scale).astype(o_ref.dtype)

    @pl.when(c >= n_qk_blocks)
    def _():
        o_ref[rows, :] = y.astype(o_ref.dtype)


def _conv_prompt_kernel(x_ref, st_ref, w_ref, o_ref, *, sub, n_q_blocks, n_qk_blocks):
    t_len, tc = x_ref.shape
    w = w_ref[...]
    r8 = lax.broadcasted_iota(I32, (SUBLANES, tc), 0)

    def body(i, carry):
        r0 = pl.multiple_of(i * sub, sub)
        cur = x_ref[pl.ds(r0, sub), :]
        p0 = pl.multiple_of(jnp.maximum(r0 - SUBLANES, 0), SUBLANES)
        prev8 = jnp.where(i == 0, st_ref[...], x_ref[pl.ds(p0, SUBLANES), :])
        acc = cur * w[CONV_W - 1:CONV_W, :]
        for k in range(1, CONV_W):
            xs = pltpu.roll(cur, k, 0)
            top = jnp.where(r8 < k, pltpu.roll(prev8, k, 0), xs[:SUBLANES])
            if sub > SUBLANES:
                xs = jnp.concatenate([top, xs[SUBLANES:]], axis=0)
            else:
                xs = top
            acc = acc + xs * w[CONV_W - 1 - k:CONV_W - k, :]
        _conv_epilogue(acc, o_ref, pl.ds(r0, sub), n_q_blocks, n_qk_blocks)
        return carry

    lax.fori_loop(0, t_len // sub, body, 0)


def _conv_sample_kernel(x_ref, st_ref, w_ref, o_ref, *, seq_len, n_q_blocks, n_qk_blocks):
    rows, tc = x_ref.shape
    assert seq_len == SUBLANES
    w = w_ref[...]
    x = x_ref[...]
    st = st_ref[...]
    t = lax.broadcasted_iota(I32, (rows, tc), 0) & (seq_len - 1)
    acc = x * w[CONV_W - 1:CONV_W, :]
    for k in range(1, CONV_W):
        xs = pltpu.roll(x, k, 0)
        ss = pltpu.roll(st, rows - (seq_len - k), 0)
        acc = acc + jnp.where(t >= k, xs, ss) * w[CONV_W - 1 - k:CONV_W - k, :]
    _conv_epilogue(acc, o_ref, slice(None), n_q_blocks, n_qk_blocks)


def gdn_conv_prompt(qkv, st8, conv_w, layer, n_seq, seq_len, key_dim):
    c_dim = qkv.shape[1]
    tc = 256
    sub = _pick(seq_len, 256)
    nqb, nqkb = key_dim // tc, 2 * key_dim // tc
    return pl.pallas_call(
        functools.partial(_conv_prompt_kernel, sub=sub, n_q_blocks=nqb, n_qk_blocks=nqkb),
        out_shape=jax.ShapeDtypeStruct((n_seq * seq_len, c_dim), BF16),
        grid=(n_seq, c_dim // tc),
        in_specs=[pl.BlockSpec((seq_len, tc), lambda b, c: (b, c)),
                  pl.BlockSpec((SUBLANES, tc), lambda b, c: (b, c)),
                  pl.BlockSpec((None, CONV_W, tc), lambda b, c: (layer, 0, c))],
        out_specs=pl.BlockSpec((seq_len, tc), lambda b, c: (b, c)),
        compiler_params=_params(("arbitrary", "arbitrary"), 32),
        name="gdn_conv_prompt",
    )(qkv, st8, conv_w)


def gdn_conv_sample(qkv, row0, st8, conv_w, layer, n_rows, seq_len, key_dim):
    c_dim = qkv.shape[1]
    tc = 512
    tr = _pick(n_rows, 256)
    assert row0 % tr == 0
    rb0 = row0 // tr
    nqb, nqkb = key_dim // tc, 2 * key_dim // tc
    return pl.pallas_call(
        functools.partial(_conv_sample_kernel, seq_len=seq_len, n_q_blocks=nqb, n_qk_blocks=nqkb),
        out_shape=jax.ShapeDtypeStruct((n_rows, c_dim), BF16),
        grid=(n_rows // tr, c_dim // tc),
        in_specs=[pl.BlockSpec((tr, tc), lambda r, c: (rb0 + r, c)),
                  pl.BlockSpec((tr, tc), lambda r, c: (r, c)),
                  pl.BlockSpec((None, CONV_W, tc), lambda r, c: (layer, 0, c))],
        out_specs=pl.BlockSpec((tr, tc), lambda r, c: (r, c)),
        compiler_params=_params(("arbitrary", "arbitrary"), 32),
        name="gdn_conv_sample",
    )(qkv, st8, conv_w)


def _dot_nt(a, b):
    return lax.dot_general(a, b, (((1,), (1,)), ((), ())), preferred_element_type=F32)


def _dot_tn(a, b):
    return lax.dot_general(a, b, (((0,), (0,)), ((), ())), preferred_element_type=F32)


def _bdot(a, b):
    return jnp.dot(a.astype(BF16), b.astype(BF16), preferred_element_type=F32)


def _chunk_setup(kk, qkm, kf, qf, vf, gb, gt, h, n_heads, masks):
    incl, strict, incl_t, same = masks
    rows = kk.shape[0]
    lane_gb = lax.broadcasted_iota(I32, (rows, 2 * n_heads), 1)
    sub_gt = lax.broadcasted_iota(I32, (n_heads, rows), 0)
    beta = jnp.sum(jnp.where(lane_gb == h, gb, 0.0), axis=-1, keepdims=True)
    g_col = jnp.sum(jnp.where(lane_gb == h + n_heads, gb, 0.0), axis=-1, keepdims=True)
    g_row = jnp.sum(jnp.where(sub_gt == h, gt, 0.0), axis=0, keepdims=True)
    gcum = jnp.sum(jnp.where(incl, g_row, 0.0), axis=-1, keepdims=True)
    gcum_row = jnp.sum(jnp.where(incl_t, g_col, 0.0), axis=0, keepdims=True)
    decay = jnp.exp(jnp.where(incl, gcum - gcum_row, -jnp.inf))
    egc = jnp.exp(gcum)
    out = dict(low=jnp.where(strict, kk * decay * beta, 0.0),
               rhs=jnp.concatenate([vf * beta, kf * (beta * egc)], axis=1),
               qk=qkm * decay, q_g=qf * egc, gcum=gcum, kf=kf)
    if same is not None:
        out["gsum"] = jnp.sum(jnp.where(same, g_row, 0.0), axis=-1, keepdims=True)
    return out


def _unit_lower_inverse(lows, eye, n_sq):
    invs = [eye - low for low in lows]
    pows = lows
    for _ in range(n_sq):
        pows = [_bdot(p, p) for p in pows]
        invs = [inv + _bdot(inv, p) for inv, p in zip(invs, pows)]
    return invs


def _delta_kernel(q_ref, k_ref, v_ref, z_ref, gb_ref, gt_ref, nw_ref, o_init_ref, o_ref, s_ref, *,
                  n_chunks, n_heads, heads_per_qk, qk_per_step):
    c_len = CHUNK_B
    hpb = pl.program_id(1)
    hps = heads_per_qk
    n_local = qk_per_step * hps

    @pl.when(pl.program_id(2) == 0)
    def _():
        s_ref[...] = jnp.zeros_like(s_ref)

    ri = lax.broadcasted_iota(I32, (c_len, c_len), 0)
    ci = lax.broadcasted_iota(I32, (c_len, c_len), 1)
    masks = (ci <= ri, ci < ri, ri <= ci, None)
    eye = (ci == ri).astype(F32)
    nw = nw_ref[...]

    prep = []
    for c in range(n_chunks):
        rs = slice(c * c_len, (c + 1) * c_len)
        gb = gb_ref[rs, :]
        gt = gt_ref[:, rs]
        for qh in range(qk_per_step):
            kc = k_ref[rs, qh * DK:(qh + 1) * DK]
            qc = q_ref[rs, qh * DK:(qh + 1) * DK]
            kk = _dot_nt(kc, kc)
            qkm = _dot_nt(qc, kc)
            kf = kc.astype(F32)
            qf = qc.astype(F32)
            for hh in range(hps):
                hl = qh * hps + hh
                vf = v_ref[rs, hl * DV:(hl + 1) * DV].astype(F32)
                prep.append(_chunk_setup(kk, qkm, kf, qf, vf, gb, gt, hpb * n_local + hl, n_heads, masks))
    invs = _unit_lower_inverse([p["low"] for p in prep], eye, c_len.bit_length() - 2)
    sols = [_bdot(inv, p["rhs"]) for inv, p in zip(invs, prep)]

    state = [s_ref[0, hl] for hl in range(n_local)]
    for c in range(n_chunks):
        rs = slice(c * c_len, (c + 1) * c_len)
        idx = [c * n_local + hl for hl in range(n_local)]
        ws = [_bdot(jnp.concatenate([sols[i][:, DV:], prep[i]["q_g"]], axis=0), state[hl])
              for hl, i in enumerate(idx)]
        u_new = [sols[i][:, :DV] - ws[hl][:c_len] for hl, i in enumerate(idx)]
        outs = [ws[hl][c_len:] + _bdot(prep[i]["qk"], u_new[hl]) for hl, i in enumerate(idx)]
        for hl, i in enumerate(idx):
            p = prep[i]
            g_last = p["gcum"][c_len - 1:c_len, :]
            k_tail = p["kf"] * jnp.exp(g_last - p["gcum"])
            state[hl] = state[hl] * jnp.exp(g_last) + _dot_tn(k_tail.astype(BF16), u_new[hl].astype(BF16))
        for hl in range(n_local):
            hs = slice(hl * DV, (hl + 1) * DV)
            zf = z_ref[rs, hs].astype(F32)
            o_ref[rs, hs] = (_rms(outs[hl], nw) * _silu(zf)).astype(o_ref.dtype)
    for hl in range(n_local):
        s_ref[0, hl] = state[hl]


def gated_delta_prompt(qkvc, z, gb, gt, norm_w, n_seq, seq_len, key_dim, n_heads):
    n_qk = key_dim // DK
    hps = n_heads // n_qk
    qps = 2 if n_qk % 2 == 0 else 1
    tc = _pick(seq_len, 256)
    nt = seq_len // tc
    qw, vw = qps * DK, qps * hps * DV
    k_blk0, v_blk0 = key_dim // qw, 2 * key_dim // vw
    rb = lambda b, t: b * nt + t
    return pl.pallas_call(
        functools.partial(_delta_kernel, n_chunks=tc // CHUNK_B, n_heads=n_heads, heads_per_qk=hps,
                          qk_per_step=qps),
        out_shape=(jax.ShapeDtypeStruct((z.shape[0], n_heads * DV), BF16),
                   jax.ShapeDtypeStruct((n_seq, n_heads, DK, DV), F32)),
        grid=(n_seq, n_qk // qps, nt),
        in_specs=[pl.BlockSpec((tc, qw), lambda b, h, t: (rb(b, t), h)),
                  pl.BlockSpec((tc, qw), lambda b, h, t: (rb(b, t), k_blk0 + h)),
                  pl.BlockSpec((tc, vw), lambda b, h, t: (rb(b, t), v_blk0 + h)),
                  pl.BlockSpec((tc, vw), lambda b, h, t: (rb(b, t), h)),
                  pl.BlockSpec((tc, 2 * n_heads), lambda b, h, t: (rb(b, t), 0)),
                  pl.BlockSpec((n_heads, tc), lambda b, h, t: (0, rb(b, t))),
                  pl.BlockSpec((1, DV), lambda b, h, t: (0, 0)),
                  pl.BlockSpec(memory_space=pl.ANY)],
        out_specs=(pl.BlockSpec((tc, vw), lambda b, h, t: (rb(b, t), h)),
                   pl.BlockSpec((1, qps * hps, DK, DV), lambda b, h, t: (b, h, 0, 0))),
        input_output_aliases={7: 0},
        compiler_params=_params(("arbitrary", "arbitrary", "arbitrary"), 32),
        name="gated_delta_prompt",
    )(qkvc, qkvc, qkvc, z, gb, gt, norm_w.reshape(1, DV), jnp.zeros((z.shape[0], n_heads * DV), BF16))


def _delta_sample_kernel(q_ref, k_ref, v_ref, z_ref, gb_ref, gt_ref, nw_ref, s0_ref, *rest,
                         seq_len, n_heads, heads_per_qk, slab):
    o_ref, s_ref = rest[-2:]
    if slab is not None:
        for other in range(s_ref.shape[0]):
            if other != slab:
                s_ref[other] = jnp.zeros(s_ref.shape[1:], F32)
        s_ref = s_ref.at[slab]
    c_len = CHUNK_B
    hpb = pl.program_id(1)
    hps = heads_per_qk
    rows = q_ref.shape[0]
    per = c_len // seq_len
    shift = seq_len.bit_length() - 1

    ri = lax.broadcasted_iota(I32, (c_len, c_len), 0)
    ci = lax.broadcasted_iota(I32, (c_len, c_len), 1)
    same = (ri >> shift) == (ci >> shift)
    masks = (same & (ci <= ri), same & (ci < ri), same & (ri <= ci), same)
    eye = (ci == ri).astype(F32)
    nw = nw_ref[...]

    prep = []
    for c in range(rows // c_len):
        rs = slice(c * c_len, (c + 1) * c_len)
        kc = k_ref[rs, :]
        qc = q_ref[rs, :]
        kk = _dot_nt(kc, kc)
        qkm = _dot_nt(qc, kc)
        kf = kc.astype(F32)
        qf = qc.astype(F32)
        gb = gb_ref[rs, :]
        gt = gt_ref[:, rs]
        for hh in range(hps):
            vf = v_ref[rs, hh * DV:(hh + 1) * DV].astype(F32)
            prep.append(_chunk_setup(kk, qkm, kf, qf, vf, gb, gt, hpb * hps + hh, n_heads, masks))
    invs = _unit_lower_inverse([p["low"] for p in prep], eye, seq_len.bit_length() - 2)
    sols = [_bdot(inv, p["rhs"]) for inv, p in zip(invs, prep)]

    for c in range(rows // c_len):
        rs = slice(c * c_len, (c + 1) * c_len)
        for hh in range(hps):
            p, sol = prep[c * hps + hh], sols[c * hps + hh]
            u_v, w_k = sol[:, :DV], sol[:, DV:]
            k_tail = p["kf"] * jnp.exp(p["gsum"] - p["gcum"])
            g_tail = jnp.exp(p["gsum"])
            s_old = [s0_ref[c * per + i, hh] for i in range(per)]
            seq = [slice(i * seq_len, (i + 1) * seq_len) for i in range(per)]
            ws = [_bdot(jnp.concatenate([w_k[r], p["q_g"][r]], axis=0), s) for r, s in zip(seq, s_old)]
            u_new = jnp.concatenate([u_v[r] - w[:seq_len] for r, w in zip(seq, ws)], axis=0)
            out = jnp.concatenate([w[seq_len:] for w in ws], axis=0) + _bdot(p["qk"], u_new)
            for i, r in enumerate(seq):
                s_ref[c * per + i, hh] = (s_old[i] * g_tail[i * seq_len:i * seq_len + 1, :]
                                          + _dot_tn(k_tail[r].astype(BF16), u_new[r].astype(BF16)))
            hs = slice(hh * DV, (hh + 1) * DV)
            zf = z_ref[rs, hs].astype(F32)
            o_ref[rs, hs] = (_rms(out, nw) * _silu(zf)).astype(o_ref.dtype)


def gated_delta_sample(qkvc, z, gb, gt, norm_w, s0, layer, row0, n_seq, seq_len, key_dim, o_all, s_all):
    n_heads = s0.shape[2]
    n_qk = key_dim // DK
    hps = n_heads // n_qk
    tr = LANES
    sps = tr // seq_len
    assert row0 % tr == 0 and n_seq % sps == 0 and seq_len == SUBLANES
    rb0 = row0 // tr
    vw = hps * DV
    v_blk0 = 2 * key_dim // vw
    args = [qkvc, qkvc, qkvc, z, gb, gt, norm_w.reshape(1, DV), s0, o_all]
    in_specs = [pl.BlockSpec((tr, DK), lambda g, h: (g, h)),
                pl.BlockSpec((tr, DK), lambda g, h: (g, n_qk + h)),
                pl.BlockSpec((tr, vw), lambda g, h: (g, v_blk0 + h)),
                pl.BlockSpec((tr, vw), lambda g, h: (rb0 + g, h)),
                pl.BlockSpec((tr, 2 * n_heads), lambda g, h: (rb0 + g, 0)),
                pl.BlockSpec((n_heads, tr), lambda g, h: (0, rb0 + g)),
                pl.BlockSpec((1, DV), lambda g, h: (0, 0)),
                pl.BlockSpec((None, sps, hps, DK, DV), lambda g, h: (layer, g, h, 0, 0)),
                pl.BlockSpec(memory_space=pl.ANY)]
    aliases = {8: 0}
    if s_all is not None:
        args.append(s_all)
        in_specs.append(pl.BlockSpec(memory_space=pl.ANY))
        aliases[9] = 1
        slab = None
        s_spec = pl.BlockSpec((None, sps, hps, DK, DV), lambda g, h: (layer, g, h, 0, 0))
    else:
        slab = layer
        s_spec = pl.BlockSpec((s0.shape[0], sps, hps, DK, DV), lambda g, h: (0, g, h, 0, 0))
    return pl.pallas_call(
        functools.partial(_delta_sample_kernel, seq_len=seq_len, n_heads=n_heads, heads_per_qk=hps, slab=slab),
        out_shape=(jax.ShapeDtypeStruct(o_all.shape, BF16), jax.ShapeDtypeStruct(s0.shape, F32)),
        grid=(n_seq // sps, n_qk),
        in_specs=in_specs,
        out_specs=(pl.BlockSpec((tr, vw), lambda g, h: (rb0 + g, h)), s_spec),
        input_output_aliases=aliases,
        compiler_params=_params(("arbitrary", "arbitrary"), 32),
        name="gated_delta_sample",
    )(*args)


def _router_kernel(x_ref, nw_ref, wr_ref, br_ref, h_ref, sel_ref, gate_ref, cnt_ref, cnt_sc):
    tm = x_ref.shape[0]

    @pl.when(pl.program_id(0) == 0)
    def _():
        cnt_sc[...] = jnp.zeros_like(cnt_sc)

    h = _rms(x_ref[...], nw_ref[...])
    h_ref[...] = h
    logits = jnp.dot(h, wr_ref[...], precision=lax.Precision.HIGHEST, preferred_element_type=F32) + br_ref[...]
    lane = lax.broadcasted_iota(I32, (tm, LANES), 1)
    lane_f = lane.astype(F32)
    big = float(LANES)

    is_g = lane < N_GROUPS
    gl = jnp.where(is_g, logits, -jnp.inf)
    ge = jnp.exp(gl - jnp.max(gl, axis=-1, keepdims=True))
    gp = ge / jnp.sum(ge, axis=-1, keepdims=True)
    p_group = jnp.max(gp, axis=-1, keepdims=True)
    g_sel = jnp.min(jnp.where(is_g & (gp == p_group), lane_f, big), axis=-1, keepdims=True)

    e_lane = lane - N_GROUPS
    in_grp = (e_lane >= 0) & (e_lane < N_EXPERTS) & ((e_lane >> 3).astype(F32) == g_sel)
    el = jnp.where(in_grp, logits, -jnp.inf)
    ee = jnp.exp(el - jnp.max(el, axis=-1, keepdims=True))
    ep = ee / jnp.sum(ee, axis=-1, keepdims=True)
    p1 = jnp.max(ep, axis=-1, keepdims=True)
    i1 = jnp.min(jnp.where(in_grp & (ep == p1), lane_f, big), axis=-1, keepdims=True)
    rest = in_grp & (lane_f != i1)
    p2 = jnp.max(jnp.where(rest, ep, -1.0), axis=-1, keepdims=True)
    i2 = jnp.min(jnp.where(rest & (ep == p2), lane_f, big), axis=-1, keepdims=True)
    denom = p1 + p2
    gate1 = p_group * p1 / denom
    gate2 = p_group * p2 / denom
    e1 = i1 - float(N_GROUPS)
    e2 = i2 - float(N_GROUPS)

    onehot = (lane_f == e1) | (lane_f == e2)
    oh = jnp.where(onehot, 1.0, 0.0)
    rr = lax.broadcasted_iota(I32, (tm, tm), 0)
    cc = lax.broadcasted_iota(I32, (tm, tm), 1)
    tri = jnp.where(cc < rr, 1.0, 0.0).astype(BF16)
    before = jnp.dot(tri, oh.astype(BF16), preferred_element_type=F32) + cnt_sc[...]
    r1 = jnp.sum(jnp.where(lane_f == e1, before, 0.0), axis=-1, keepdims=True)
    r2 = jnp.sum(jnp.where(lane_f == e2, before, 0.0), axis=-1, keepdims=True)
    cnt_sc[...] = cnt_sc[...] + jnp.sum(oh, axis=0, keepdims=True)
    cnt_ref[...] = cnt_sc[...]

    sel = jnp.where(lane == 0, e1, jnp.where(lane == 1, e2, jnp.where(lane == 2, r1, jnp.where(lane == 3, r2, 0.0))))
    sel_ref[...] = sel.astype(I32)
    gate_ref[...] = jnp.where(lane == 0, gate1, jnp.where(lane == 1, gate2, 0.0))


def moe_router(x, norm_w, w_route, b_route):
    n, d = x.shape
    tm = _pick(n, 256)
    return pl.pallas_call(
        _router_kernel,
        out_shape=(jax.ShapeDtypeStruct((n, d), F32),
                   jax.ShapeDtypeStruct((n, LANES), I32),
                   jax.ShapeDtypeStruct((n, LANES), F32),
                   jax.ShapeDtypeStruct((1, LANES), F32)),
        grid=(n // tm,),
        in_specs=[pl.BlockSpec((tm, d), lambda i: (i, 0)),
                  pl.BlockSpec((1, d), lambda i: (0, 0)),
                  pl.BlockSpec((d, LANES), lambda i: (0, 0)),
                  pl.BlockSpec((1, LANES), lambda i: (0, 0))],
        out_specs=(pl.BlockSpec((tm, d), lambda i: (i, 0)),
                   pl.BlockSpec((tm, LANES), lambda i: (i, 0)),
                   pl.BlockSpec((tm, LANES), lambda i: (i, 0)),
                   pl.BlockSpec((1, LANES), lambda i: (0, 0))),
        scratch_shapes=[pltpu.VMEM((1, LANES), F32)],
        compiler_params=_params(("arbitrary",), 32),
        name="moe_router",
    )(x, norm_w.reshape(1, d), w_route, b_route)


SEL_W = 2 * TOP_K


def _slot(sel_ref, pstart_ref, r, k):
    return pstart_ref[sel_ref[0, 0, SEL_W * r + k]] + sel_ref[0, 0, SEL_W * r + TOP_K + k]


def _row_copy_out(h_ref, xs_ref, sem, r, d):
    return pltpu.make_async_copy(h_ref.at[pl.ds(r, 1), :], xs_ref.at[pl.ds(d, 1), :], sem)


def _dispatch_kernel(sel_ref, pstart_ref, h_ref, xs_in_ref, xs_ref, sem):
    del xs_in_ref
    tm = h_ref.shape[0]

    def issue(r, carry):
        for k in range(TOP_K):
            _row_copy_out(h_ref, xs_ref, sem, r, _slot(sel_ref, pstart_ref, r, k)).start()
        return carry

    lax.fori_loop(0, tm, issue, 0)

    def drain(r, carry):
        for k in range(TOP_K):
            _row_copy_out(h_ref, xs_ref, sem, r, _slot(sel_ref, pstart_ref, r, k)).wait()
        return carry

    lax.fori_loop(0, tm, drain, 0)


def moe_dispatch(h, sel3, pstart, n_slots):
    n, d = h.shape
    tm = sel3.shape[2] // SEL_W
    xs0 = jnp.zeros((n_slots, d), F32)
    return pl.pallas_call(
        _dispatch_kernel,
        out_shape=jax.ShapeDtypeStruct((n_slots, d), F32),
        grid=(n // tm,),
        in_specs=[pl.BlockSpec((1, 1, SEL_W * tm), lambda i: (i, 0, 0), memory_space=pltpu.SMEM),
                  pl.BlockSpec(memory_space=pltpu.SMEM),
                  pl.BlockSpec((tm, d), lambda i: (i, 0)),
                  pl.BlockSpec(memory_space=pl.ANY)],
        out_specs=pl.BlockSpec(memory_space=pl.ANY),
        scratch_shapes=[pltpu.SemaphoreType.DMA(())],
        input_output_aliases={3: 0},
        compiler_params=_params(("arbitrary",), 32),
        name="moe_dispatch",
    )(sel3, pstart, h, xs0)


def _ffn_kernel(be_ref, first_ref, nused_ref, x_ref, wg_ref, wu_ref, wd_ref, o_ref, wgb, wub, wdb):
    b = pl.program_id(0)
    valid = b < nused_ref[0]

    @pl.when(valid & (first_ref[b] == 1))
    def _():
        wgb[...] = wg_ref[...].astype(BF16)
        wub[...] = wu_ref[...].astype(BF16)
        wdb[...] = wd_ref[...].astype(BF16)

    @pl.when(valid)
    def _():
        x = x_ref[...].astype(BF16)
        gate = jnp.dot(x, wgb[...], preferred_element_type=F32)
        up = jnp.dot(x, wub[...], preferred_element_type=F32)
        hid = (_silu(gate) * up).astype(BF16)
        o_ref[...] = jnp.dot(hid, wdb[...], preferred_element_type=F32)

    @pl.when(jnp.logical_not(valid))
    def _():
        o_ref[...] = jnp.zeros_like(o_ref)


def moe_experts(xs, block_expert, block_first, n_used, w_gate, w_up, w_down, layer):
    n_slots, d = xs.shape
    de = w_gate.shape[-1]
    n_blocks = n_slots // MOE_ROWS
    xmap = lambda b, be, fi, nu: (jnp.minimum(b, nu[0] - 1), 0)
    grid_spec = pltpu.PrefetchScalarGridSpec(
        num_scalar_prefetch=3,
        grid=(n_blocks,),
        in_specs=[pl.BlockSpec((MOE_ROWS, d), xmap),
                  pl.BlockSpec((None, None, d, de), lambda b, be, fi, nu: (layer, be[b], 0, 0)),
                  pl.BlockSpec((None, None, d, de), lambda b, be, fi, nu: (layer, be[b], 0, 0)),
                  pl.BlockSpec((None, None, de, d), lambda b, be, fi, nu: (layer, be[b], 0, 0))],
        out_specs=pl.BlockSpec((MOE_ROWS, d), lambda b, be, fi, nu: (b, 0)),
        scratch_shapes=[pltpu.VMEM((d, de), BF16), pltpu.VMEM((d, de), BF16), pltpu.VMEM((de, d), BF16)],
    )
    return pl.pallas_call(
        _ffn_kernel,
        out_shape=jax.ShapeDtypeStruct((n_slots, d), F32),
        grid_spec=grid_spec,
        compiler_params=_params(("arbitrary",), 48),
        name="moe_experts",
    )(block_expert, block_first, n_used, xs, w_gate, w_up, w_down)


def _row_copy_in(yb_ref, buf_ref, sem, k, r, d):
    return pltpu.make_async_copy(yb_ref.at[pl.ds(d, 1), :], buf_ref.at[k, pl.ds(r, 1), :], sem)


def _combine_kernel(sel_ref, pstart_ref, x_ref, gate_ref, nw_ref, yb_ref, xo_ref, ho_ref, buf_ref, sem):
    tm = x_ref.shape[0]

    def issue(r, carry):
        for k in range(TOP_K):
            _row_copy_in(yb_ref, buf_ref, sem, k, r, _slot(sel_ref, pstart_ref, r, k)).start()
        return carry

    lax.fori_loop(0, tm, issue, 0)

    def drain(r, carry):
        for k in range(TOP_K):
            _row_copy_in(yb_ref, buf_ref, sem, k, r, _slot(sel_ref, pstart_ref, r, k)).wait()
        return carry

    lax.fori_loop(0, tm, drain, 0)

    g = gate_ref[...]
    y = buf_ref[0] * g[:, 0:1] + buf_ref[1] * g[:, 1:2]
    xn = x_ref[...] + y
    xo_ref[...] = xn
    ho_ref[...] = _rms(xn, nw_ref[...]).astype(ho_ref.dtype)


def moe_combine(x, yb, sel3, pstart, gates, next_norm_w, h_dtype):
    n, d = x.shape
    tm = sel3.shape[2] // SEL_W
    return pl.pallas_call(
        _combine_kernel,
        out_shape=(jax.ShapeDtypeStruct((n, d), F32), jax.ShapeDtypeStruct((n, d), h_dtype)),
        grid=(n // tm,),
        in_specs=[pl.BlockSpec((1, 1, SEL_W * tm), lambda i: (i, 0, 0), memory_space=pltpu.SMEM),
                  pl.BlockSpec(memory_space=pltpu.SMEM),
                  pl.BlockSpec((tm, d), lambda i: (i, 0)),
                  pl.BlockSpec((tm, LANES), lambda i: (i, 0)),
                  pl.BlockSpec((1, d), lambda i: (0, 0)),
                  pl.BlockSpec(memory_space=pl.ANY)],
        out_specs=(pl.BlockSpec((tm, d), lambda i: (i, 0)),
                   pl.BlockSpec((tm, d), lambda i: (i, 0))),
        scratch_shapes=[pltpu.VMEM((TOP_K, tm, d), F32), pltpu.SemaphoreType.DMA(())],
        compiler_params=_params(("arbitrary",), 40),
        name="moe_combine",
    )(sel3, pstart, x, gates, next_norm_w.reshape(1, d), yb)


def hier_moe(x, ffn_norm_w, w_route, b_route, w_gate, w_up, w_down, layer, next_norm_w, h_dtype):
    n, _ = x.shape
    h, sel, gates, cnt = moe_router(x, ffn_norm_w, w_route, b_route)
    counts = cnt[0, :N_EXPERTS].astype(I32)
    padded = (counts + MOE_ROWS - 1) // MOE_ROWS * MOE_ROWS
    pend = jnp.cumsum(padded)
    pstart = (pend - padded).astype(I32)
    tm = _pick(n, 256)
    sel3 = sel[:, :SEL_W].reshape(n // tm, 1, SEL_W * tm)
    n_blocks = -(-(n * TOP_K) // MOE_ROWS) + N_EXPERTS
    n_used = pend[-1] // MOE_ROWS
    blk = jnp.arange(n_blocks, dtype=I32)
    last = jnp.maximum(n_used - 1, 0)
    bstart = jnp.minimum(blk, last) * MOE_ROWS
    block_expert = jnp.minimum(jnp.sum(pend[None, :] <= bstart[:, None], axis=1), N_EXPERTS - 1).astype(I32)
    block_first = (bstart == pstart[block_expert]).astype(I32)
    xs = moe_dispatch(h, sel3, pstart, n_blocks * MOE_ROWS)
    yb = moe_experts(xs, block_expert, block_first, n_used.reshape(1).astype(I32), w_gate, w_up, w_down, layer)
    return moe_combine(x, yb, sel3, pstart, gates, next_norm_w, h_dtype)


def kernel(x_prompt, x_sample, state_delta_S, state_delta_conv, norm_mixer_w, norm_ffn_w, norm_final_w,
           a_w_in, a_ln_g, a_ln_b, a_w_s, a_b_s, a_w_out,
           b_w_in, b_conv_w, b_a_log, b_dt_bias, b_norm_w, b_w_out,
           moe_w_group, moe_b_group, moe_w_router, moe_b_router, moe_w_gate, moe_w_up, moe_w_down):
    nb, seq, d = x_prompt.shape
    db, dseq, _ = x_sample.shape
    depth = norm_mixer_w.shape[0]
    n_p, n_s = nb * seq, db * dseq
    n = n_p + n_s
    a_width = a_ln_g.shape[1]
    n_groups_a = a_width // A_GROUP_DIM
    n_heads = b_a_log.shape[1]
    value_dim = n_heads * DV
    conv_dim = b_conv_w.shape[2]
    key_dim = (conv_dim - value_dim) // 2
    assert seq % CHUNK_A == 0 and n_s % CHUNK_A == 0 and CHUNK_A % dseq == 0 and dseq == SUBLANES
    assert seq % CHUNK_B == 0 and dseq <= CHUNK_B and dseq >= CONV_W - 1

    x = jnp.concatenate([x_prompt.reshape(n_p, d), x_sample.reshape(n_s, d)], axis=0)

    causal = jnp.tril(jnp.ones((CHUNK_A, CHUNK_A), bool))
    ws_p = jnp.where(causal, a_w_s, 0.0)
    per = CHUNK_A // dseq
    ws_s = jnp.where(causal[:dseq, :dseq], a_w_s[:, :, :dseq, :dseq], 0.0)
    ws_s = jnp.einsum("ij,lgts->lgitjs", jnp.eye(per, dtype=F32), ws_s).reshape(a_w_s.shape)
    ws2 = jnp.stack([ws_p, ws_s], axis=1).astype(BF16)
    bias_p = jnp.repeat(jnp.swapaxes(a_b_s, 1, 2), A_GROUP_DIM, axis=2)
    bias_s = jnp.tile(bias_p[:, :dseq], (1, per, 1))
    bias2 = jnp.stack([bias_p, bias_s], axis=1)

    w_route = jnp.concatenate([moe_w_group, moe_w_router,
                               jnp.zeros((depth, d, LANES - N_GROUPS - N_EXPERTS), F32)], axis=2)
    b_route = jnp.concatenate([moe_b_group, moe_b_router,
                               jnp.zeros((depth, LANES - N_GROUPS - N_EXPERTS), F32)], axis=1)
    b_w_in_t = jnp.swapaxes(b_w_in, 1, 2)
    zeros_h = jnp.zeros_like(b_a_log)
    a_log2 = jnp.concatenate([zeros_h, b_a_log], axis=1)
    dt2 = jnp.concatenate([zeros_h, b_dt_bias], axis=1)

    chunk_v, s_prompt, conv_prompt, conv_sample = [], [], [], []
    s_sample = None
    h = rmsnorm(x, norm_mixer_w[0], BF16)
    for i in range(depth):
        j = i // 2
        if i % 2 == 0:
            u = matmul(h, a_w_in, j, 0, a_width, BF16, act="gelu")
            vpre = matmul(h, a_w_in, j, a_width, a_width, F32, act="gelu")
            gated, v_s = gmlp_gate(u, vpre, a_ln_g[j], a_ln_b[j], ws2[j], bias2[j], n_p // CHUNK_A)
            chunk_v.append(v_s.reshape(db, dseq, a_width))
            x = matmul(gated, a_w_out, j, 0, d, F32, res=x)
        else:
            qkv = matmul(h, b_w_in_t, j, 0, conv_dim, F32, w_t=True)
            z = matmul(h, b_w_in_t, j, conv_dim, value_dim, BF16, w_t=True)
            gb = matmul(h, b_w_in_t, j, conv_dim + value_dim, 2 * n_heads, F32, act="gdn_gates",
                        extra=(a_log2[j:j + 1], dt2[j:j + 1]), w_t=True)
            gt = gb[:, n_heads:].T
            keep = CONV_W - 1
            conv_prompt.append(jnp.stack([qkv[(b + 1) * seq - keep:(b + 1) * seq] for b in range(nb)]))
            conv_sample.append(qkv[n_p:].reshape(db, dseq, conv_dim)[:, dseq - keep:])
            pad8 = ((0, 0), (SUBLANES - keep, 0), (0, 0))
            st8_p = jnp.zeros((nb * SUBLANES, conv_dim), F32)
            st8_s = jnp.pad(state_delta_conv[j], pad8).reshape(db * SUBLANES, conv_dim)
            qc_p = gdn_conv_prompt(qkv, st8_p, b_conv_w, j, nb, seq, key_dim)
            qc_s = gdn_conv_sample(qkv, n_p, st8_s, b_conv_w, j, n_s, dseq, key_dim)
            o_p, s_p = gated_delta_prompt(qc_p, z, gb, gt, b_norm_w[j], nb, seq, key_dim, n_heads)
            gated, s_sample = gated_delta_sample(qc_s, z, gb, gt, b_norm_w[j], state_delta_S, j, n_p, db, dseq,
                                                 key_dim, o_p, s_sample)
            s_prompt.append(s_p)
            x = matmul(gated, b_w_out, j, 0, d, F32, res=x)
        last = i == depth - 1
        next_w = norm_final_w if last else norm_mixer_w[i + 1]
        x, h = hier_moe(x, norm_ffn_w[i], w_route[i], b_route[i:i + 1], moe_w_gate, moe_w_up, moe_w_down, i,
                        next_w, F32 if last else BF16)

    y_prompt = h[:n_p].reshape(nb, seq, d)
    y_sample = h[n_p:].reshape(db, dseq, d)
    return (y_prompt, y_sample, jnp.stack(chunk_v), jnp.stack(s_prompt), jnp.stack(conv_prompt),
            s_sample, jnp.stack(conv_sample))
```

```python
import functools

import jax
import jax.numpy as jnp
from jax import lax
from jax.experimental import pallas as pl
from jax.experimental.pallas import tpu as pltpu

F32 = jnp.float32
BF16 = jnp.bfloat16
I32 = jnp.int32

EPS = 1e-6
LANES = 128
SUBLANES = 8
CHUNK_A = 128
A_GROUP_DIM = 128
DK = 128
DV = 128
CONV_W = 4
CHUNK_B = 64
N_GROUPS = 8
EXPERTS_PER_GROUP = 8
N_EXPERTS = N_GROUPS * EXPERTS_PER_GROUP
TOP_K = 2
MOE_ROWS = 256
MIB = 2 ** 20


def _params(semantics, vmem_mib):
    return pltpu.CompilerParams(dimension_semantics=semantics, vmem_limit_bytes=vmem_mib * MIB)


def _pick(n, pref):
    t = min(n, pref)
    while n % t:
        t -= SUBLANES
    return t


def _rms(x, w):
    return x * lax.rsqrt(jnp.mean(x * x, axis=-1, keepdims=True) + EPS) * w


def _silu(x):
    return x * jax.nn.sigmoid(x)


def _rmsnorm_kernel(x_ref, w_ref, o_ref):
    o_ref[...] = _rms(x_ref[...], w_ref[...]).astype(o_ref.dtype)


def rmsnorm(x, w, out_dtype):
    n, d = x.shape
    tm = _pick(n, 512)
    return pl.pallas_call(
        _rmsnorm_kernel,
        out_shape=jax.ShapeDtypeStruct((n, d), out_dtype),
        grid=(n // tm,),
        in_specs=[pl.BlockSpec((tm, d), lambda i: (i, 0)),
                  pl.BlockSpec((1, d), lambda i: (0, 0))],
        out_specs=pl.BlockSpec((tm, d), lambda i: (i, 0)),
        compiler_params=_params(("arbitrary",), 32),
        name="rmsnorm",
    )(x, w.reshape(1, d))


def _mm_kernel(*refs, act, has_res, n_extra, w_t):
    a_ref, w_ref = refs[:2]
    extra = refs[2:2 + n_extra]
    rest = refs[2 + n_extra:]
    if has_res:
        r_ref, o_ref, wb_ref = rest
    else:
        o_ref, wb_ref = rest

    @pl.when(pl.program_id(1) == 0)
    def _():
        wb_ref[...] = w_ref[...].astype(BF16)

    if w_t:
        acc = _dot_nt(a_ref[...], wb_ref[...])
    else:
        acc = jnp.dot(a_ref[...], wb_ref[...], preferred_element_type=F32)
    if act == "gelu":
        acc = jax.nn.gelu(acc)
    elif act == "gdn_gates":
        a_log_ref, dt_ref = extra
        h = acc.shape[1] // 2
        lane = lax.broadcasted_iota(I32, acc.shape, 1)
        g = -jnp.exp(a_log_ref[...]) * jax.nn.softplus(acc + dt_ref[...])
        acc = jnp.where(lane < h, jax.nn.sigmoid(acc), g)
    if has_res:
        acc = acc + r_ref[...]
    o_ref[...] = acc.astype(o_ref.dtype)


def matmul(a, w, layer, col0, ncols, out_dtype, *, act=None, res=None, extra=(), tm=1024, tn=512, w_t=False):
    m, k = a.shape
    tm = _pick(m, tm)
    if k <= 2048 and ncols % (2 * tn) == 0 and col0 % (2 * tn) == 0:
        tn = 2 * tn
    tn = min(tn, ncols)
    assert ncols % tn == 0 and col0 % tn == 0
    c0 = col0 // tn
    if w_t:
        w_spec = pl.BlockSpec((None, tn, k), lambda j, i: (layer, c0 + j, 0))
    else:
        w_spec = pl.BlockSpec((None, k, tn), lambda j, i: (layer, 0, c0 + j))
    in_specs = [pl.BlockSpec((tm, k), lambda j, i: (i, 0)), w_spec]
    args = [a, w]
    for e in extra:
        in_specs.append(pl.BlockSpec((1, tn), lambda j, i: (0, j)))
        args.append(e)
    if res is not None:
        in_specs.append(pl.BlockSpec((tm, tn), lambda j, i: (i, j)))
        args.append(res)
    out_bytes = jnp.dtype(out_dtype).itemsize
    vmem = (2 * tm * k * 2 + 2 * k * tn * 4 + k * tn * 2 + 2 * tm * tn * out_bytes
            + (2 * tm * tn * 4 if res is not None else 0) + 3 * tm * tn * 4)
    return pl.pallas_call(
        functools.partial(_mm_kernel, act=act, has_res=res is not None, n_extra=len(extra), w_t=w_t),
        out_shape=jax.ShapeDtypeStruct((m, ncols), out_dtype),
        grid=(ncols // tn, m // tm),
        in_specs=in_specs,
        out_specs=pl.BlockSpec((tm, tn), lambda j, i: (i, j)),
        scratch_shapes=[pltpu.VMEM((tn, k) if w_t else (k, tn), BF16)],
        compiler_params=_params(("arbitrary", "arbitrary"), min(56, vmem // MIB + 8)),
        name="matmul_" + (act or "plain") + ("_res" if res is not None else ""),
    )(*args)


def _gmlp_gate_kernel(u_ref, v_ref, g_ref, b_ref, ws_ref, bias_ref, o_ref, vo_ref, *, n_groups, first_sample):
    v = v_ref[...]
    xc = v - jnp.mean(v, axis=-1, keepdims=True)
    var = jnp.mean(xc * xc, axis=-1, keepdims=True)
    vn = xc * lax.rsqrt(var + EPS) * g_ref[...] + b_ref[...]

    @pl.when(pl.program_id(0) >= first_sample)
    def _():
        vo_ref[...] = vn

    vb = vn.astype(BF16)
    for g in range(n_groups):
        sl = slice(g * A_GROUP_DIM, (g + 1) * A_GROUP_DIM)
        s = jnp.dot(ws_ref[g], vb[:, sl], preferred_element_type=F32) + bias_ref[:, sl]
        o_ref[:, sl] = (u_ref[:, sl].astype(F32) * s).astype(BF16)


def gmlp_gate(u, vpre, ln_g, ln_b, ws2, bias2, n_prompt_chunks):
    n, aw = u.shape
    n_chunks = n // CHUNK_A
    n_groups = aw // A_GROUP_DIM
    n_sample_rows = n - n_prompt_chunks * CHUNK_A
    which = lambda c: jnp.where(c >= n_prompt_chunks, 1, 0)
    return pl.pallas_call(
        functools.partial(_gmlp_gate_kernel, n_groups=n_groups, first_sample=n_prompt_chunks),
        out_shape=(jax.ShapeDtypeStruct((n, aw), BF16),
                   jax.ShapeDtypeStruct((n_sample_rows, aw), F32)),
        grid=(n_chunks,),
        in_specs=[pl.BlockSpec((CHUNK_A, aw), lambda c: (c, 0)),
                  pl.BlockSpec((CHUNK_A, aw), lambda c: (c, 0)),
                  pl.BlockSpec((1, aw), lambda c: (0, 0)),
                  pl.BlockSpec((1, aw), lambda c: (0, 0)),
                  pl.BlockSpec((None, n_groups, CHUNK_A, CHUNK_A), lambda c: (which(c), 0, 0, 0)),
                  pl.BlockSpec((None, CHUNK_A, aw), lambda c: (which(c), 0, 0))],
        out_specs=(pl.BlockSpec((CHUNK_A, aw), lambda c: (c, 0)),
                   pl.BlockSpec((CHUNK_A, aw), lambda c: (jnp.maximum(c - n_prompt_chunks, 0), 0))),
        compiler_params=_params(("arbitrary",), 40),
        name="gmlp_gate",
    )(u, vpre, ln_g.reshape(1, aw), ln_b.reshape(1, aw), ws2, bias2)


def _conv_epilogue(y, o_ref, rows, n_q_blocks, n_qk_blocks):
    c = pl.program_id(1)
    y = _silu(y)
    tc = y.shape[1]

    @pl.when(c < n_qk_blocks)
    def _():
        scale = jnp.where(c < n_q_blocks, DK ** -0.5, 1.0).astype(F32)
        for j in range(tc // DK):
            blk = y[:, j * DK:(j + 1) * DK]
            nrm = blk * lax.rsqrt(jnp.sum(blk * blk, axis=-1, keepdims=True) + EPS)
            o_ref[rows, j * DK:(j + 1) * DK] = (nrm * scale).astype(o_ref.dtype)

    @pl.when(c >= n_qk_blocks)
    def _():
        o_ref[rows, :] = y.astype(o_ref.dtype)


def _conv_prompt_kernel(x_ref, st_ref, w_ref, o_ref, *, sub, n_q_blocks, n_qk_blocks):
    t_len, tc = x_ref.shape
    w = w_ref[...]
    r8 = lax.broadcasted_iota(I32, (SUBLANES, tc), 0)

    def body(i, carry):
        r0 = pl.multiple_of(i * sub, sub)
        cur = x_ref[pl.ds(r0, sub), :]
        p0 = pl.multiple_of(jnp.maximum(r0 - SUBLANES, 0), SUBLANES)
        prev8 = jnp.where(i == 0, st_ref[...], x_ref[pl.ds(p0, SUBLANES), :])
        acc = cur * w[CONV_W - 1:CONV_W, :]
        for k in range(1, CONV_W):
            xs = pltpu.roll(cur, k, 0)
            top = jnp.where(r8 < k, pltpu.roll(prev8, k, 0), xs[:SUBLANES])
            if sub > SUBLANES:
                xs = jnp.concatenate([top, xs[SUBLANES:]], axis=0)
            else:
                xs = top
            acc = acc + xs * w[CONV_W - 1 - k:CONV_W - k, :]
        _conv_epilogue(acc, o_ref, pl.ds(r0, sub), n_q_blocks, n_qk_blocks)
        return carry

    lax.fori_loop(0, t_len // sub, body, 0)


def _conv_sample_kernel(x_ref, st_ref, w_ref, o_ref, *, seq_len, n_q_blocks, n_qk_blocks):
    rows, tc = x_ref.shape
    assert seq_len == SUBLANES
    w = w_ref[...]
    x = x_ref[...]
    st = st_ref[...]
    t = lax.broadcasted_iota(I32, (rows, tc), 0) & (seq_len - 1)
    acc = x * w[CONV_W - 1:CONV_W, :]
    for k in range(1, CONV_W):
        xs = pltpu.roll(x, k, 0)
        ss = pltpu.roll(st, rows - (seq_len - k), 0)
        acc = acc + jnp.where(t >= k, xs, ss) * w[CONV_W - 1 - k:CONV_W - k, :]
    _conv_epilogue(acc, o_ref, slice(None), n_q_blocks, n_qk_blocks)


def gdn_conv_prompt(qkv, st8, conv_w, layer, n_seq, seq_len, key_dim):
    c_dim = qkv.shape[1]
    tc = 256
    sub = _pick(seq_len, 128)
    nqb, nqkb = key_dim // tc, 2 * key_dim // tc
    return pl.pallas_call(
        functools.partial(_conv_prompt_kernel, sub=sub, n_q_blocks=nqb, n_qk_blocks=nqkb),
        out_shape=jax.ShapeDtypeStruct((n_seq * seq_len, c_dim), BF16),
        grid=(n_seq, c_dim // tc),
        in_specs=[pl.BlockSpec((seq_len, tc), lambda b, c: (b, c)),
                  pl.BlockSpec((SUBLANES, tc), lambda b, c: (b, c)),
                  pl.BlockSpec((None, CONV_W, tc), lambda b, c: (layer, 0, c))],
        out_specs=pl.BlockSpec((seq_len, tc), lambda b, c: (b, c)),
        compiler_params=_params(("arbitrary", "arbitrary"), 32),
        name="gdn_conv_prompt",
    )(qkv, st8, conv_w)


def gdn_conv_sample(qkv, row0, st8, conv_w, layer, n_rows, seq_len, key_dim):
    c_dim = qkv.shape[1]
    tc = 512
    tr = _pick(n_rows, 256)
    assert row0 % tr == 0
    rb0 = row0 // tr
    nqb, nqkb = key_dim // tc, 2 * key_dim // tc
    return pl.pallas_call(
        functools.partial(_conv_sample_kernel, seq_len=seq_len, n_q_blocks=nqb, n_qk_blocks=nqkb),
        out_shape=jax.ShapeDtypeStruct((n_rows, c_dim), BF16),
        grid=(n_rows // tr, c_dim // tc),
        in_specs=[pl.BlockSpec((tr, tc), lambda r, c: (rb0 + r, c)),
                  pl.BlockSpec((tr, tc), lambda r, c: (r, c)),
                  pl.BlockSpec((None, CONV_W, tc), lambda r, c: (layer, 0, c))],
        out_specs=pl.BlockSpec((tr, tc), lambda r, c: (r, c)),
        compiler_params=_params(("arbitrary", "arbitrary"), 32),
        name="gdn_conv_sample",
    )(qkv, st8, conv_w)


def _dot_nt(a, b):
    return lax.dot_general(a, b, (((1,), (1,)), ((), ())), preferred_element_type=F32)


def _dot_tn(a, b):
    return lax.dot_general(a, b, (((0,), (0,)), ((), ())), preferred_element_type=F32)


def _bdot(a, b):
    return jnp.dot(a.astype(BF16), b.astype(BF16), preferred_element_type=F32)


def _chunk_setup(kk, qkm, kf, qf, vf, gb, gt, h, n_heads, masks):
    incl, strict, incl_t, same = masks
    rows = kk.shape[0]
    lane_gb = lax.broadcasted_iota(I32, (rows, 2 * n_heads), 1)
    sub_gt = lax.broadcasted_iota(I32, (n_heads, rows), 0)
    beta = jnp.sum(jnp.where(lane_gb == h, gb, 0.0), axis=-1, keepdims=True)
    g_col = jnp.sum(jnp.where(lane_gb == h + n_heads, gb, 0.0), axis=-1, keepdims=True)
    g_row = jnp.sum(jnp.where(sub_gt == h, gt, 0.0), axis=0, keepdims=True)
    gcum = jnp.sum(jnp.where(incl, g_row, 0.0), axis=-1, keepdims=True)
    gcum_row = jnp.sum(jnp.where(incl_t, g_col, 0.0), axis=0, keepdims=True)
    decay = jnp.exp(jnp.where(incl, gcum - gcum_row, -jnp.inf))
    egc = jnp.exp(gcum)
    out = dict(low=jnp.where(strict, kk * decay * beta, 0.0),
               rhs=jnp.concatenate([vf * beta, kf * (beta * egc)], axis=1),
               qk=qkm * decay, q_g=qf * egc, gcum=gcum, kf=kf)
    if same is not None:
        out["gsum"] = jnp.sum(jnp.where(same, g_row, 0.0), axis=-1, keepdims=True)
    return out


def _unit_lower_inverse(lows, eye, n_sq):
    invs = [eye - low for low in lows]
    pows = lows
    for _ in range(n_sq):
        pows = [_bdot(p, p) for p in pows]
        invs = [inv + _bdot(inv, p) for inv, p in zip(invs, pows)]
    return invs


def _delta_kernel(q_ref, k_ref, v_ref, z_ref, gb_ref, gt_ref, nw_ref, o_init_ref, o_ref, s_ref, *,
                  n_chunks, n_heads, heads_per_qk, qk_per_step):
    c_len = CHUNK_B
    hpb = pl.program_id(1)
    hps = heads_per_qk
    n_local = qk_per_step * hps

    @pl.when(pl.program_id(2) == 0)
    def _():
        s_ref[...] = jnp.zeros_like(s_ref)

    ri = lax.broadcasted_iota(I32, (c_len, c_len), 0)
    ci = lax.broadcasted_iota(I32, (c_len, c_len), 1)
    masks = (ci <= ri, ci < ri, ri <= ci, None)
    eye = (ci == ri).astype(F32)
    nw = nw_ref[...]

    prep = []
    for c in range(n_chunks):
        rs = slice(c * c_len, (c + 1) * c_len)
        gb = gb_ref[rs, :]
        gt = gt_ref[:, rs]
        for qh in range(qk_per_step):
            kc = k_ref[rs, qh * DK:(qh + 1) * DK]
            qc = q_ref[rs, qh * DK:(qh + 1) * DK]
            kk = _dot_nt(kc, kc)
            qkm = _dot_nt(qc, kc)
            kf = kc.astype(F32)
            qf = qc.astype(F32)
            for hh in range(hps):
                hl = qh * hps + hh
                vf = v_ref[rs, hl * DV:(hl + 1) * DV].astype(F32)
                prep.append(_chunk_setup(kk, qkm, kf, qf, vf, gb, gt, hpb * n_local + hl, n_heads, masks))
    invs = _unit_lower_inverse([p["low"] for p in prep], eye, c_len.bit_length() - 2)
    sols = [_bdot(inv, p["rhs"]) for inv, p in zip(invs, prep)]

    state = [s_ref[0, hl] for hl in range(n_local)]
    for c in range(n_chunks):
        rs = slice(c * c_len, (c + 1) * c_len)
        idx = [c * n_local + hl for hl in range(n_local)]
        ws = [_bdot(jnp.concatenate([sols[i][:, DV:], prep[i]["q_g"]], axis=0), state[hl])
              for hl, i in enumerate(idx)]
        u_new = [sols[i][:, :DV] - ws[hl][:c_len] for hl, i in enumerate(idx)]
        outs = [ws[hl][c_len:] + _bdot(prep[i]["qk"], u_new[hl]) for hl, i in enumerate(idx)]
        for hl, i in enumerate(idx):
            p = prep[i]
            g_last = p["gcum"][c_len - 1:c_len, :]
            k_tail = p["kf"] * jnp.exp(g_last - p["gcum"])
            state[hl] = state[hl] * jnp.exp(g_last) + _dot_tn(k_tail.astype(BF16), u_new[hl].astype(BF16))
        for hl in range(n_local):
            hs = slice(hl * DV, (hl + 1) * DV)
            zf = z_ref[rs, hs].astype(F32)
            o_ref[rs, hs] = (_rms(outs[hl], nw) * _silu(zf)).astype(o_ref.dtype)
    for hl in range(n_local):
        s_ref[0, hl] = state[hl]


def gated_delta_prompt(qkvc, z, gb, gt, norm_w, n_seq, seq_len, key_dim, n_heads):
    n_qk = key_dim // DK
    hps = n_heads // n_qk
    qps = 4 if n_qk % 4 == 0 else 1
    tc = _pick(seq_len, 256)
    nt = seq_len // tc
    qw, vw = qps * DK, qps * hps * DV
    k_blk0, v_blk0 = key_dim // qw, 2 * key_dim // vw
    rb = lambda b, t: b * nt + t
    return pl.pallas_call(
        functools.partial(_delta_kernel, n_chunks=tc // CHUNK_B, n_heads=n_heads, heads_per_qk=hps,
                          qk_per_step=qps),
        out_shape=(jax.ShapeDtypeStruct((z.shape[0], n_heads * DV), BF16),
                   jax.ShapeDtypeStruct((n_seq, n_heads, DK, DV), F32)),
        grid=(n_seq, n_qk // qps, nt),
        in_specs=[pl.BlockSpec((tc, qw), lambda b, h, t: (rb(b, t), h)),
                  pl.BlockSpec((tc, qw), lambda b, h, t: (rb(b, t), k_blk0 + h)),
                  pl.BlockSpec((tc, vw), lambda b, h, t: (rb(b, t), v_blk0 + h)),
                  pl.BlockSpec((tc, vw), lambda b, h, t: (rb(b, t), h)),
                  pl.BlockSpec((tc, 2 * n_heads), lambda b, h, t: (rb(b, t), 0)),
                  pl.BlockSpec((n_heads, tc), lambda b, h, t: (0, rb(b, t))),
                  pl.BlockSpec((1, DV), lambda b, h, t: (0, 0)),
                  pl.BlockSpec(memory_space=pl.ANY)],
        out_specs=(pl.BlockSpec((tc, vw), lambda b, h, t: (rb(b, t), h)),
                   pl.BlockSpec((1, qps * hps, DK, DV), lambda b, h, t: (b, h, 0, 0))),
        input_output_aliases={7: 0},
        compiler_params=_params(("arbitrary", "arbitrary", "arbitrary"), 32),
        name="gated_delta_prompt",
    )(qkvc, qkvc, qkvc, z, gb, gt, norm_w.reshape(1, DV), jnp.zeros((z.shape[0], n_heads * DV), BF16))


def _delta_sample_kernel(q_ref, k_ref, v_ref, z_ref, gb_ref, gt_ref, nw_ref, s0_ref, *rest,
                         seq_len, n_heads, heads_per_qk, slab):
    o_ref, s_ref = rest[-2:]
    if slab is not None:
        for other in range(s_ref.shape[0]):
            if other != slab:
                s_ref[other] = jnp.zeros(s_ref.shape[1:], F32)
        s_ref = s_ref.at[slab]
    c_len = CHUNK_B
    hpb = pl.program_id(1)
    hps = heads_per_qk
    rows = q_ref.shape[0]
    per = c_len // seq_len
    shift = seq_len.bit_length() - 1

    ri = lax.broadcasted_iota(I32, (c_len, c_len), 0)
    ci = lax.broadcasted_iota(I32, (c_len, c_len), 1)
    same = (ri >> shift) == (ci >> shift)
    masks = (same & (ci <= ri), same & (ci < ri), same & (ri <= ci), same)
    eye = (ci == ri).astype(F32)
    nw = nw_ref[...]

    prep = []
    for c in range(rows // c_len):
        rs = slice(c * c_len, (c + 1) * c_len)
        kc = k_ref[rs, :]
        qc = q_ref[rs, :]
        kk = _dot_nt(kc, kc)
        qkm = _dot_nt(qc, kc)
        kf = kc.astype(F32)
        qf = qc.astype(F32)
        gb = gb_ref[rs, :]
        gt = gt_ref[:, rs]
        for hh in range(hps):
            vf = v_ref[rs, hh * DV:(hh + 1) * DV].astype(F32)
            prep.append(_chunk_setup(kk, qkm, kf, qf, vf, gb, gt, hpb * hps + hh, n_heads, masks))
    invs = _unit_lower_inverse([p["low"] for p in prep], eye, seq_len.bit_length() - 2)
    sols = [_bdot(inv, p["rhs"]) for inv, p in zip(invs, prep)]

    for c in range(rows // c_len):
        rs = slice(c * c_len, (c + 1) * c_len)
        for hh in range(hps):
            p, sol = prep[c * hps + hh], sols[c * hps + hh]
            u_v, w_k = sol[:, :DV], sol[:, DV:]
            k_tail = p["kf"] * jnp.exp(p["gsum"] - p["gcum"])
            g_tail = jnp.exp(p["gsum"])
            s_old = [s0_ref[c * per + i, hh] for i in range(per)]
            seq = [slice(i * seq_len, (i + 1) * seq_len) for i in range(per)]
            ws = [_bdot(jnp.concatenate([w_k[r], p["q_g"][r]], axis=0), s) for r, s in zip(seq, s_old)]
            u_new = jnp.concatenate([u_v[r] - w[:seq_len] for r, w in zip(seq, ws)], axis=0)
            out = jnp.concatenate([w[seq_len:] for w in ws], axis=0) + _bdot(p["qk"], u_new)
            for i, r in enumerate(seq):
                s_ref[c * per + i, hh] = (s_old[i] * g_tail[i * seq_len:i * seq_len + 1, :]
                                          + _dot_tn(k_tail[r].astype(BF16), u_new[r].astype(BF16)))
            hs = slice(hh * DV, (hh + 1) * DV)
            zf = z_ref[rs, hs].astype(F32)
            o_ref[rs, hs] = (_rms(out, nw) * _silu(zf)).astype(o_ref.dtype)


def gated_delta_sample(qkvc, z, gb, gt, norm_w, s0, layer, row0, n_seq, seq_len, key_dim, o_all, s_all):
    n_heads = s0.shape[2]
    n_qk = key_dim // DK
    hps = n_heads // n_qk
    tr = LANES
    sps = tr // seq_len
    assert row0 % tr == 0 and n_seq % sps == 0 and seq_len == SUBLANES
    rb0 = row0 // tr
    vw = hps * DV
    v_blk0 = 2 * key_dim // vw
    args = [qkvc, qkvc, qkvc, z, gb, gt, norm_w.reshape(1, DV), s0, o_all]
    in_specs = [pl.BlockSpec((tr, DK), lambda g, h: (g, h)),
                pl.BlockSpec((tr, DK), lambda g, h: (g, n_qk + h)),
                pl.BlockSpec((tr, vw), lambda g, h: (g, v_blk0 + h)),
                pl.BlockSpec((tr, vw), lambda g, h: (rb0 + g, h)),
                pl.BlockSpec((tr, 2 * n_heads), lambda g, h: (rb0 + g, 0)),
                pl.BlockSpec((n_heads, tr), lambda g, h: (0, rb0 + g)),
                pl.BlockSpec((1, DV), lambda g, h: (0, 0)),
                pl.BlockSpec((None, sps, hps, DK, DV), lambda g, h: (layer, g, h, 0, 0)),
                pl.BlockSpec(memory_space=pl.ANY)]
    aliases = {8: 0}
    if s_all is not None:
        args.append(s_all)
        in_specs.append(pl.BlockSpec(memory_space=pl.ANY))
        aliases[9] = 1
        slab = None
        s_spec = pl.BlockSpec((None, sps, hps, DK, DV), lambda g, h: (layer, g, h, 0, 0))
    else:
        slab = layer
        s_spec = pl.BlockSpec((s0.shape[0], sps, hps, DK, DV), lambda g, h: (0, g, h, 0, 0))
    return pl.pallas_call(
        functools.partial(_delta_sample_kernel, seq_len=seq_len, n_heads=n_heads, heads_per_qk=hps, slab=slab),
        out_shape=(jax.ShapeDtypeStruct(o_all.shape, BF16), jax.ShapeDtypeStruct(s0.shape, F32)),
        grid=(n_seq // sps, n_qk),
        in_specs=in_specs,
        out_specs=(pl.BlockSpec((tr, vw), lambda g, h: (rb0 + g, h)), s_spec),
        input_output_aliases=aliases,
        compiler_params=_params(("arbitrary", "arbitrary"), 32),
        name="gated_delta_sample",
    )(*args)


def _router_kernel(x_ref, nw_ref, wr_ref, br_ref, h_ref, sel_ref, gate_ref, cnt_ref, cnt_sc):
    tm = x_ref.shape[0]

    @pl.when(pl.program_id(0) == 0)
    def _():
        cnt_sc[...] = jnp.zeros_like(cnt_sc)

    h = _rms(x_ref[...], nw_ref[...])
    h_ref[...] = h
    logits = jnp.dot(h, wr_ref[...], precision=lax.Precision.HIGHEST, preferred_element_type=F32) + br_ref[...]
    lane = lax.broadcasted_iota(I32, (tm, LANES), 1)
    lane_f = lane.astype(F32)
    big = float(LANES)

    is_g = lane < N_GROUPS
    gl = jnp.where(is_g, logits, -jnp.inf)
    ge = jnp.exp(gl - jnp.max(gl, axis=-1, keepdims=True))
    gp = ge / jnp.sum(ge, axis=-1, keepdims=True)
    p_group = jnp.max(gp, axis=-1, keepdims=True)
    g_sel = jnp.min(jnp.where(is_g & (gp == p_group), lane_f, big), axis=-1, keepdims=True)

    e_lane = lane - N_GROUPS
    in_grp = (e_lane >= 0) & (e_lane < N_EXPERTS) & ((e_lane >> 3).astype(F32) == g_sel)
    el = jnp.where(in_grp, logits, -jnp.inf)
    ee = jnp.exp(el - jnp.max(el, axis=-1, keepdims=True))
    ep = ee / jnp.sum(ee, axis=-1, keepdims=True)
    p1 = jnp.max(ep, axis=-1, keepdims=True)
    i1 = jnp.min(jnp.where(in_grp & (ep == p1), lane_f, big), axis=-1, keepdims=True)
    rest = in_grp & (lane_f != i1)
    p2 = jnp.max(jnp.where(rest, ep, -1.0), axis=-1, keepdims=True)
    i2 = jnp.min(jnp.where(rest & (ep == p2), lane_f, big), axis=-1, keepdims=True)
    denom = p1 + p2
    gate1 = p_group * p1 / denom
    gate2 = p_group * p2 / denom
    e1 = i1 - float(N_GROUPS)
    e2 = i2 - float(N_GROUPS)

    onehot = (lane_f == e1) | (lane_f == e2)
    oh = jnp.where(onehot, 1.0, 0.0)
    rr = lax.broadcasted_iota(I32, (tm, tm), 0)
    cc = lax.broadcasted_iota(I32, (tm, tm), 1)
    tri = jnp.where(cc < rr, 1.0, 0.0).astype(BF16)
    before = jnp.dot(tri, oh.astype(BF16), preferred_element_type=F32) + cnt_sc[...]
    r1 = jnp.sum(jnp.where(lane_f == e1, before, 0.0), axis=-1, keepdims=True)
    r2 = jnp.sum(jnp.where(lane_f == e2, before, 0.0), axis=-1, keepdims=True)
    cnt_sc[...] = cnt_sc[...] + jnp.sum(oh, axis=0, keepdims=True)
    cnt_ref[...] = cnt_sc[...]

    sel = jnp.where(lane == 0, e1, jnp.where(lane == 1, e2, jnp.where(lane == 2, r1, jnp.where(lane == 3, r2, 0.0))))
    sel_ref[...] = sel.astype(I32)
    gate_ref[...] = jnp.where(lane == 0, gate1, jnp.where(lane == 1, gate2, 0.0))


def moe_router(x, norm_w, w_route, b_route):
    n, d = x.shape
    tm = _pick(n, 256)
    return pl.pallas_call(
        _router_kernel,
        out_shape=(jax.ShapeDtypeStruct((n, d), F32),
                   jax.ShapeDtypeStruct((n, LANES), I32),
                   jax.ShapeDtypeStruct((n, LANES), F32),
                   jax.ShapeDtypeStruct((1, LANES), F32)),
        grid=(n // tm,),
        in_specs=[pl.BlockSpec((tm, d), lambda i: (i, 0)),
                  pl.BlockSpec((1, d), lambda i: (0, 0)),
                  pl.BlockSpec((d, LANES), lambda i: (0, 0)),
                  pl.BlockSpec((1, LANES), lambda i: (0, 0))],
        out_specs=(pl.BlockSpec((tm, d), lambda i: (i, 0)),
                   pl.BlockSpec((tm, LANES), lambda i: (i, 0)),
                   pl.BlockSpec((tm, LANES), lambda i: (i, 0)),
                   pl.BlockSpec((1, LANES), lambda i: (0, 0))),
        scratch_shapes=[pltpu.VMEM((1, LANES), F32)],
        compiler_params=_params(("arbitrary",), 32),
        name="moe_router",
    )(x, norm_w.reshape(1, d), w_route, b_route)


SEL_W = 2 * TOP_K
ISSUE_UNROLL = 8


def _slot(sel_ref, pstart_ref, r, k):
    return pstart_ref[sel_ref[0, 0, SEL_W * r + k]] + sel_ref[0, 0, SEL_W * r + TOP_K + k]


def _row_copy_out(h_ref, xs_ref, sem, r, d):
    return pltpu.make_async_copy(h_ref.at[pl.ds(r, 1), :], xs_ref.at[pl.ds(d, 1), :], sem)


def _dispatch_kernel(sel_ref, pstart_ref, h_ref, xs_in_ref, xs_ref, sem):
    del xs_in_ref
    tm = h_ref.shape[0]

    def issue(r, carry):
        for k in range(TOP_K):
            _row_copy_out(h_ref, xs_ref, sem, r, _slot(sel_ref, pstart_ref, r, k)).start()
        return carry

    lax.fori_loop(0, tm, issue, 0, unroll=ISSUE_UNROLL)
    for k in range(TOP_K):
        pltpu.make_async_copy(h_ref, xs_ref.at[pl.ds(0, tm), :], sem).wait()


def moe_dispatch(h, sel3, pstart, n_slots):
    n, d = h.shape
    tm = sel3.shape[2] // SEL_W
    xs0 = jnp.zeros((n_slots, d), F32)
    return pl.pallas_call(
        _dispatch_kernel,
        out_shape=jax.ShapeDtypeStruct((n_slots, d), F32),
        grid=(n // tm,),
        in_specs=[pl.BlockSpec((1, 1, SEL_W * tm), lambda i: (i, 0, 0), memory_space=pltpu.SMEM),
                  pl.BlockSpec(memory_space=pltpu.SMEM),
                  pl.BlockSpec((tm, d), lambda i: (i, 0)),
                  pl.BlockSpec(memory_space=pl.ANY)],
        out_specs=pl.BlockSpec(memory_space=pl.ANY),
        scratch_shapes=[pltpu.SemaphoreType.DMA(())],
        input_output_aliases={3: 0},
        compiler_params=_params(("arbitrary",), 32),
        name="moe_dispatch",
    )(sel3, pstart, h, xs0)


def _ffn_kernel(be_ref, first_ref, nused_ref, x_ref, wg_ref, wu_ref, wd_ref, o_ref, wgb, wub, wdb):
    b = pl.program_id(0)
    valid = b < nused_ref[0]

    @pl.when(valid & (first_ref[b] == 1))
    def _():
        wgb[...] = wg_ref[...].astype(BF16)
        wub[...] = wu_ref[...].astype(BF16)
        wdb[...] = wd_ref[...].astype(BF16)

    @pl.when(valid)
    def _():
        x = x_ref[...].astype(BF16)
        gate = jnp.dot(x, wgb[...], preferred_element_type=F32)
        up = jnp.dot(x, wub[...], preferred_element_type=F32)
        hid = (_silu(gate) * up).astype(BF16)
        o_ref[...] = jnp.dot(hid, wdb[...], preferred_element_type=F32)

    @pl.when(jnp.logical_not(valid))
    def _():
        o_ref[...] = jnp.zeros_like(o_ref)


def moe_experts(xs, block_expert, block_first, n_used, w_gate, w_up, w_down, layer):
    n_slots, d = xs.shape
    de = w_gate.shape[-1]
    n_blocks = n_slots // MOE_ROWS
    xmap = lambda b, be, fi, nu: (jnp.minimum(b, jnp.maximum(nu[0] - 1, 0)), 0)
    grid_spec = pltpu.PrefetchScalarGridSpec(
        num_scalar_prefetch=3,
        grid=(n_blocks,),
        in_specs=[pl.BlockSpec((MOE_ROWS, d), xmap),
                  pl.BlockSpec((None, None, d, de), lambda b, be, fi, nu: (layer, be[b], 0, 0)),
                  pl.BlockSpec((None, None, d, de), lambda b, be, fi, nu: (layer, be[b], 0, 0)),
                  pl.BlockSpec((None, None, de, d), lambda b, be, fi, nu: (layer, be[b], 0, 0))],
        out_specs=pl.BlockSpec((MOE_ROWS, d), lambda b, be, fi, nu: (b, 0)),
        scratch_shapes=[pltpu.VMEM((d, de), BF16), pltpu.VMEM((d, de), BF16), pltpu.VMEM((de, d), BF16)],
    )
    return pl.pallas_call(
        _ffn_kernel,
        out_shape=jax.ShapeDtypeStruct((n_slots, d), F32),
        grid_spec=grid_spec,
        compiler_params=_params(("arbitrary",), 48),
        name="moe_experts",
    )(block_expert, block_first, n_used, xs, w_gate, w_up, w_down)


def _row_copy_in(yb_ref, buf_ref, sem, k, r, d):
    return pltpu.make_async_copy(yb_ref.at[pl.ds(d, 1), :], buf_ref.at[k, pl.ds(r, 1), :], sem)


def _combine_kernel(sel_ref, pstart_ref, x_ref, gate_ref, nw_ref, yb_ref, xo_ref, ho_ref, buf_ref, sem):
    tm = x_ref.shape[0]

    def issue(r, carry):
        for k in range(TOP_K):
            _row_copy_in(yb_ref, buf_ref, sem, k, r, _slot(sel_ref, pstart_ref, r, k)).start()
        return carry

    lax.fori_loop(0, tm, issue, 0, unroll=ISSUE_UNROLL)
    for k in range(TOP_K):
        pltpu.make_async_copy(yb_ref.at[pl.ds(0, tm), :], buf_ref.at[k], sem).wait()

    g = gate_ref[...]
    y = buf_ref[0] * g[:, 0:1] + buf_ref[1] * g[:, 1:2]
    xn = x_ref[...] + y
    xo_ref[...] = xn
    ho_ref[...] = _rms(xn, nw_ref[...]).astype(ho_ref.dtype)


def moe_combine(x, yb, sel3, pstart, gates, next_norm_w, h_dtype):
    n, d = x.shape
    tm = sel3.shape[2] // SEL_W
    return pl.pallas_call(
        _combine_kernel,
        out_shape=(jax.ShapeDtypeStruct((n, d), F32), jax.ShapeDtypeStruct((n, d), h_dtype)),
        grid=(n // tm,),
        in_specs=[pl.BlockSpec((1, 1, SEL_W * tm), lambda i: (i, 0, 0), memory_space=pltpu.SMEM),
                  pl.BlockSpec(memory_space=pltpu.SMEM),
                  pl.BlockSpec((tm, d), lambda i: (i, 0)),
                  pl.BlockSpec((tm, LANES), lambda i: (i, 0)),
                  pl.BlockSpec((1, d), lambda i: (0, 0)),
                  pl.BlockSpec(memory_space=pl.ANY)],
        out_specs=(pl.BlockSpec((tm, d), lambda i: (i, 0)),
                   pl.BlockSpec((tm, d), lambda i: (i, 0))),
        scratch_shapes=[pltpu.VMEM((TOP_K, tm, d), F32), pltpu.SemaphoreType.DMA(())],
        compiler_params=_params(("arbitrary",), 40),
        name="moe_combine",
    )(sel3, pstart, x, gates, next_norm_w.reshape(1, d), yb)


def hier_moe(x, ffn_norm_w, w_route, b_route, w_gate, w_up, w_down, layer, next_norm_w, h_dtype):
    n, _ = x.shape
    h, sel, gates, cnt = moe_router(x, ffn_norm_w, w_route, b_route)
    counts = cnt[0, :N_EXPERTS].astype(I32)
    padded = (counts + MOE_ROWS - 1) // MOE_ROWS * MOE_ROWS
    pend = jnp.cumsum(padded)
    pstart = (pend - padded).astype(I32)
    tm = _pick(n, 256)
    sel3 = sel[:, :SEL_W].reshape(n // tm, 1, SEL_W * tm)
    n_blocks = -(-(n * TOP_K) // MOE_ROWS) + N_EXPERTS
    n_used = pend[-1] // MOE_ROWS
    blk = jnp.arange(n_blocks, dtype=I32)
    last = jnp.maximum(n_used - 1, 0)
    bstart = jnp.minimum(blk, last) * MOE_ROWS
    block_expert = jnp.minimum(jnp.sum(pend[None, :] <= bstart[:, None], axis=1), N_EXPERTS - 1).astype(I32)
    block_first = (bstart == pstart[block_expert]).astype(I32)
    xs = moe_dispatch(h, sel3, pstart, n_blocks * MOE_ROWS)
    yb = moe_experts(xs, block_expert, block_first, n_used.reshape(1).astype(I32), w_gate, w_up, w_down, layer)
    return moe_combine(x, yb, sel3, pstart, gates, next_norm_w, h_dtype)


def kernel(x_prompt, x_sample, state_delta_S, state_delta_conv, norm_mixer_w, norm_ffn_w, norm_final_w,
           a_w_in, a_ln_g, a_ln_b, a_w_s, a_b_s, a_w_out,
           b_w_in, b_conv_w, b_a_log, b_dt_bias, b_norm_w, b_w_out,
           moe_w_group, moe_b_group, moe_w_router, moe_b_router, moe_w_gate, moe_w_up, moe_w_down):
    nb, seq, d = x_prompt.shape
    db, dseq, _ = x_sample.shape
    depth = norm_mixer_w.shape[0]
    n_p, n_s = nb * seq, db * dseq
    n = n_p + n_s
    a_width = a_ln_g.shape[1]
    n_groups_a = a_width // A_GROUP_DIM
    n_heads = b_a_log.shape[1]
    value_dim = n_heads * DV
    conv_dim = b_conv_w.shape[2]
    key_dim = (conv_dim - value_dim) // 2
    assert seq % CHUNK_A == 0 and n_s % CHUNK_A == 0 and CHUNK_A % dseq == 0 and dseq == SUBLANES
    assert seq % CHUNK_B == 0 and dseq <= CHUNK_B and dseq >= CONV_W - 1

    x = jnp.concatenate([x_prompt.reshape(n_p, d), x_sample.reshape(n_s, d)], axis=0)

    causal = jnp.tril(jnp.ones((CHUNK_A, CHUNK_A), bool))
    ws_p = jnp.where(causal, a_w_s, 0.0)
    per = CHUNK_A // dseq
    ws_s = jnp.where(causal[:dseq, :dseq], a_w_s[:, :, :dseq, :dseq], 0.0)
    ws_s = jnp.einsum("ij,lgts->lgitjs", jnp.eye(per, dtype=F32), ws_s).reshape(a_w_s.shape)
    ws2 = jnp.stack([ws_p, ws_s], axis=1).astype(BF16)
    bias_p = jnp.repeat(jnp.swapaxes(a_b_s, 1, 2), A_GROUP_DIM, axis=2)
    bias_s = jnp.tile(bias_p[:, :dseq], (1, per, 1))
    bias2 = jnp.stack([bias_p, bias_s], axis=1)

    w_route = jnp.concatenate([moe_w_group, moe_w_router,
                               jnp.zeros((depth, d, LANES - N_GROUPS - N_EXPERTS), F32)], axis=2)
    b_route = jnp.concatenate([moe_b_group, moe_b_router,
                               jnp.zeros((depth, LANES - N_GROUPS - N_EXPERTS), F32)], axis=1)
    b_w_in_t = jnp.swapaxes(b_w_in, 1, 2)
    zeros_h = jnp.zeros_like(b_a_log)
    a_log2 = jnp.concatenate([zeros_h, b_a_log], axis=1)
    dt2 = jnp.concatenate([zeros_h, b_dt_bias], axis=1)

    chunk_v, s_prompt, conv_prompt, conv_sample = [], [], [], []
    s_sample = None
    h = rmsnorm(x, norm_mixer_w[0], BF16)
    for i in range(depth):
        j = i // 2
        if i % 2 == 0:
            u = matmul(h, a_w_in, j, 0, a_width, BF16, act="gelu")
            vpre = matmul(h, a_w_in, j, a_width, a_width, F32, act="gelu")
            gated, v_s = gmlp_gate(u, vpre, a_ln_g[j], a_ln_b[j], ws2[j], bias2[j], n_p // CHUNK_A)
            chunk_v.append(v_s.reshape(db, dseq, a_width))
            x = matmul(gated, a_w_out, j, 0, d, F32, res=x)
        else:
            qkv = matmul(h, b_w_in_t, j, 0, conv_dim, F32, w_t=True)
            z = matmul(h, b_w_in_t, j, conv_dim, value_dim, BF16, w_t=True)
            gb = matmul(h, b_w_in_t, j, conv_dim + value_dim, 2 * n_heads, F32, act="gdn_gates",
                        extra=(a_log2[j:j + 1], dt2[j:j + 1]), w_t=True)
            gt = gb[:, n_heads:].T
            keep = CONV_W - 1
            conv_prompt.append(jnp.stack([qkv[(b + 1) * seq - keep:(b + 1) * seq] for b in range(nb)]))
            conv_sample.append(qkv[n_p:].reshape(db, dseq, conv_dim)[:, dseq - keep:])
            pad8 = ((0, 0), (SUBLANES - keep, 0), (0, 0))
            st8_p = jnp.zeros((nb * SUBLANES, conv_dim), F32)
            st8_s = jnp.pad(state_delta_conv[j], pad8).reshape(db * SUBLANES, conv_dim)
            qc_p = gdn_conv_prompt(qkv, st8_p, b_conv_w, j, nb, seq, key_dim)
            qc_s = gdn_conv_sample(qkv, n_p, st8_s, b_conv_w, j, n_s, dseq, key_dim)
            o_p, s_p = gated_delta_prompt(qc_p, z, gb, gt, b_norm_w[j], nb, seq, key_dim, n_heads)
            gated, s_sample = gated_delta_sample(qc_s, z, gb, gt, b_norm_w[j], state_delta_S, j, n_p, db, dseq,
                                                 key_dim, o_p, s_sample)
            s_prompt.append(s_p)
            x = matmul(gated, b_w_out, j, 0, d, F32, res=x)
        last = i == depth - 1
        next_w = norm_final_w if last else norm_mixer_w[i + 1]
        x, h = hier_moe(x, norm_ffn_w[i], w_route[i], b_route[i:i + 1], moe_w_gate, moe_w_up, moe_w_down, i,
                        next_w, F32 if last else BF16)

    y_prompt = h[:n_p].reshape(nb, seq, d)
    y_sample = h[n_p:].reshape(db, dseq, d)
    return (y_prompt, y_sample, jnp.stack(chunk_v), jnp.stack(s_prompt), jnp.stack(conv_prompt),
            s_sample, jnp.stack(conv_sample))
```

```python
import functools
import math

import jax
import jax.numpy as jnp
from jax import lax
from jax.experimental import pallas as pl
from jax.experimental.pallas import tpu as pltpu

F32 = jnp.float32
BF16 = jnp.bfloat16
I32 = jnp.int32

EPS = 1e-6
LANES = 128
SUBLANES = 8
CHUNK_A = 128
A_GROUP_DIM = 128
DK = 128
DV = 128
CONV_W = 4
CHUNK_B = 64
N_GROUPS = 8
EXPERTS_PER_GROUP = 8
N_EXPERTS = N_GROUPS * EXPERTS_PER_GROUP
TOP_K = 2
MOE_ROWS = 256
MIB = 2 ** 20


def _params(semantics, vmem_mib):
    return pltpu.CompilerParams(dimension_semantics=semantics, vmem_limit_bytes=vmem_mib * MIB)


def _pick(n, pref):
    t = min(n, pref)
    while n % t:
        t -= SUBLANES
    return t


def _rms(x, w):
    return x * lax.rsqrt(jnp.mean(x * x, axis=-1, keepdims=True) + EPS) * w


def _silu(x):
    return x * jax.nn.sigmoid(x)


def _concat_rmsnorm_kernel(xa_ref, xb_ref, w_ref, x_ref, h_ref, *, steps_a):
    @pl.when(pl.program_id(0) < steps_a)
    def _():
        x_ref[...] = xa_ref[...]

    @pl.when(pl.program_id(0) >= steps_a)
    def _():
        x_ref[...] = xb_ref[...]

    h_ref[...] = _rms(x_ref[...], w_ref[...]).astype(h_ref.dtype)


def concat_rmsnorm(xa, xb, w):
    (na, d), (nb_, _) = xa.shape, xb.shape
    tm = _pick(math.gcd(na, nb_), 512)
    steps_a = na // tm
    return pl.pallas_call(
        functools.partial(_concat_rmsnorm_kernel, steps_a=steps_a),
        out_shape=(jax.ShapeDtypeStruct((na + nb_, d), F32), jax.ShapeDtypeStruct((na + nb_, d), BF16)),
        grid=((na + nb_) // tm,),
        in_specs=[pl.BlockSpec((tm, d), lambda i: (jnp.minimum(i, steps_a - 1), 0)),
                  pl.BlockSpec((tm, d), lambda i: (jnp.maximum(i - steps_a, 0), 0)),
                  pl.BlockSpec((1, d), lambda i: (0, 0))],
        out_specs=(pl.BlockSpec((tm, d), lambda i: (i, 0)), pl.BlockSpec((tm, d), lambda i: (i, 0))),
        compiler_params=_params(("arbitrary",), 32),
        name="concat_rmsnorm",
    )(xa, xb, w.reshape(1, d))


def _mm_kernel(*refs, act, has_res, n_extra, w_t):
    a_ref, w_ref = refs[:2]
    extra = refs[2:2 + n_extra]
    rest = refs[2 + n_extra:]
    if has_res:
        r_ref, o_ref, wb_ref = rest
    else:
        o_ref, wb_ref = rest

    @pl.when(pl.program_id(1) == 0)
    def _():
        wb_ref[...] = w_ref[...].astype(BF16)

    if w_t:
        acc = _dot_nt(a_ref[...], wb_ref[...])
    else:
        acc = jnp.dot(a_ref[...], wb_ref[...], preferred_element_type=F32)
    if act == "gelu":
        acc = jax.nn.gelu(acc)
    elif act == "gdn_gates":
        a_log_ref, dt_ref = extra
        h = acc.shape[1] // 2
        lane = lax.broadcasted_iota(I32, acc.shape, 1)
        g = -jnp.exp(a_log_ref[...]) * jax.nn.softplus(acc + dt_ref[...])
        acc = jnp.where(lane < h, jax.nn.sigmoid(acc), g)
    if has_res:
        acc = acc + r_ref[...]
    o_ref[...] = acc.astype(o_ref.dtype)


def matmul(a, w, layer, col0, ncols, out_dtype, *, act=None, res=None, extra=(), tm=1024, tn=512, w_t=False):
    m, k = a.shape
    tm = _pick(m, tm)
    if k <= 2048 and ncols % (2 * tn) == 0 and col0 % (2 * tn) == 0:
        tn = 2 * tn
    tn = min(tn, ncols)
    assert ncols % tn == 0 and col0 % tn == 0
    c0 = col0 // tn
    if w_t:
        w_spec = pl.BlockSpec((None, tn, k), lambda j, i: (layer, c0 + j, 0))
    else:
        w_spec = pl.BlockSpec((None, k, tn), lambda j, i: (layer, 0, c0 + j))
    in_specs = [pl.BlockSpec((tm, k), lambda j, i: (i, 0)), w_spec]
    args = [a, w]
    for e in extra:
        in_specs.append(pl.BlockSpec((1, tn), lambda j, i: (0, j)))
        args.append(e)
    if res is not None:
        in_specs.append(pl.BlockSpec((tm, tn), lambda j, i: (i, j)))
        args.append(res)
    out_bytes = jnp.dtype(out_dtype).itemsize
    vmem = (2 * tm * k * 2 + 2 * k * tn * 4 + k * tn * 2 + 2 * tm * tn * out_bytes
            + (2 * tm * tn * 4 if res is not None else 0) + 3 * tm * tn * 4)
    return pl.pallas_call(
        functools.partial(_mm_kernel, act=act, has_res=res is not None, n_extra=len(extra), w_t=w_t),
        out_shape=jax.ShapeDtypeStruct((m, ncols), out_dtype),
        grid=(ncols // tn, m // tm),
        in_specs=in_specs,
        out_specs=pl.BlockSpec((tm, tn), lambda j, i: (i, j)),
        scratch_shapes=[pltpu.VMEM((tn, k) if w_t else (k, tn), BF16)],
        compiler_params=_params(("arbitrary", "arbitrary"), min(56, vmem // MIB + 8)),
        name="matmul_" + (act or "plain") + ("_res" if res is not None else ""),
    )(*args)


def _gmlp_gate_kernel(u_ref, v_ref, g_ref, b_ref, ws_ref, bias_ref, o_ref, vo_ref, *, n_groups, first_sample):
    v = v_ref[...]
    xc = v - jnp.mean(v, axis=-1, keepdims=True)
    var = jnp.mean(xc * xc, axis=-1, keepdims=True)
    vn = xc * lax.rsqrt(var + EPS) * g_ref[...] + b_ref[...]

    @pl.when(pl.program_id(0) >= first_sample)
    def _():
        vo_ref[...] = vn

    vb = vn.astype(BF16)
    for g in range(n_groups):
        sl = slice(g * A_GROUP_DIM, (g + 1) * A_GROUP_DIM)
        s = jnp.dot(ws_ref[g], vb[:, sl], preferred_element_type=F32) + bias_ref[:, sl]
        o_ref[:, sl] = (u_ref[:, sl].astype(F32) * s).astype(BF16)


def gmlp_gate(u, vpre, ln_g, ln_b, ws2, bias2, n_prompt_chunks):
    n, aw = u.shape
    n_chunks = n // CHUNK_A
    n_groups = aw // A_GROUP_DIM
    n_sample_rows = n - n_prompt_chunks * CHUNK_A
    which = lambda c: jnp.where(c >= n_prompt_chunks, 1, 0)
    return pl.pallas_call(
        functools.partial(_gmlp_gate_kernel, n_groups=n_groups, first_sample=n_prompt_chunks),
        out_shape=(jax.ShapeDtypeStruct((n, aw), BF16),
                   jax.ShapeDtypeStruct((n_sample_rows, aw), F32)),
        grid=(n_chunks,),
        in_specs=[pl.BlockSpec((CHUNK_A, aw), lambda c: (c, 0)),
                  pl.BlockSpec((CHUNK_A, aw), lambda c: (c, 0)),
                  pl.BlockSpec((1, aw), lambda c: (0, 0)),
                  pl.BlockSpec((1, aw), lambda c: (0, 0)),
                  pl.BlockSpec((None, n_groups, CHUNK_A, CHUNK_A), lambda c: (which(c), 0, 0, 0)),
                  pl.BlockSpec((None, CHUNK_A, aw), lambda c: (which(c), 0, 0))],
        out_specs=(pl.BlockSpec((CHUNK_A, aw), lambda c: (c, 0)),
                   pl.BlockSpec((CHUNK_A, aw), lambda c: (jnp.maximum(c - n_prompt_chunks, 0), 0))),
        compiler_params=_params(("arbitrary",), 40),
        name="gmlp_gate",
    )(u, vpre, ln_g.reshape(1, aw), ln_b.reshape(1, aw), ws2, bias2)


def _conv_epilogue(y, o_ref, rows, n_q_blocks, n_qk_blocks):
    c = pl.program_id(1)
    y = _silu(y)
    tc = y.shape[1]

    @pl.when(c < n_qk_blocks)
    def _():
        scale = jnp.where(c < n_q_blocks, DK ** -0.5, 1.0).astype(F32)
        for j in range(tc // DK):
            blk = y[:, j * DK:(j + 1) * DK]
            nrm = blk * lax.rsqrt(jnp.sum(blk * blk, axis=-1, keepdims=True) + EPS)
            o_ref[rows, j * DK:(j + 1) * DK] = (nrm * scale).astype(o_ref.dtype)

    @pl.when(c >= n_qk_blocks)
    def _():
        o_ref[rows, :] = y.astype(o_ref.dtype)


def _conv_prompt_kernel(x_ref, st_ref, w_ref, o_ref, *, sub, n_q_blocks, n_qk_blocks):
    t_len, tc = x_ref.shape
    w = w_ref[...]
    r8 = lax.broadcasted_iota(I32, (SUBLANES, tc), 0)

    def body(i, carry):
        r0 = pl.multiple_of(i * sub, sub)
        cur = x_ref[pl.ds(r0, sub), :]
        p0 = pl.multiple_of(jnp.maximum(r0 - SUBLANES, 0), SUBLANES)
        prev8 = jnp.where(i == 0, st_ref[...], x_ref[pl.ds(p0, SUBLANES), :])
        acc = cur * w[CONV_W - 1:CONV_W, :]
        for k in range(1, CONV_W):
            xs = pltpu.roll(cur, k, 0)
            top = jnp.where(r8 < k, pltpu.roll(prev8, k, 0), xs[:SUBLANES])
            if sub > SUBLANES:
                xs = jnp.concatenate([top, xs[SUBLANES:]], axis=0)
            else:
                xs = top
            acc = acc + xs * w[CONV_W - 1 - k:CONV_W - k, :]
        _conv_epilogue(acc, o_ref, pl.ds(r0, sub), n_q_blocks, n_qk_blocks)
        return carry

    lax.fori_loop(0, t_len // sub, body, 0)


def _conv_sample_kernel(x_ref, st_ref, w_ref, o_ref, *, seq_len, n_q_blocks, n_qk_blocks):
    rows, tc = x_ref.shape
    assert seq_len == SUBLANES
    w = w_ref[...]
    x = x_ref[...]
    st = st_ref[...]
    t = lax.broadcasted_iota(I32, (rows, tc), 0) & (seq_len - 1)
    acc = x * w[CONV_W - 1:CONV_W, :]
    for k in range(1, CONV_W):
        xs = pltpu.roll(x, k, 0)
        ss = pltpu.roll(st, rows - (seq_len - k), 0)
        acc = acc + jnp.where(t >= k, xs, ss) * w[CONV_W - 1 - k:CONV_W - k, :]
    _conv_epilogue(acc, o_ref, slice(None), n_q_blocks, n_qk_blocks)


def gdn_conv_prompt(qkv, st8, conv_w, layer, n_seq, seq_len, key_dim):
    c_dim = qkv.shape[1]
    tc = 256
    sub = _pick(seq_len, 128)
    nqb, nqkb = key_dim // tc, 2 * key_dim // tc
    return pl.pallas_call(
        functools.partial(_conv_prompt_kernel, sub=sub, n_q_blocks=nqb, n_qk_blocks=nqkb),
        out_shape=jax.ShapeDtypeStruct((n_seq * seq_len, c_dim), BF16),
        grid=(n_seq, c_dim // tc),
        in_specs=[pl.BlockSpec((seq_len, tc), lambda b, c: (b, c)),
                  pl.BlockSpec((SUBLANES, tc), lambda b, c: (b, c)),
                  pl.BlockSpec((None, CONV_W, tc), lambda b, c: (layer, 0, c))],
        out_specs=pl.BlockSpec((seq_len, tc), lambda b, c: (b, c)),
        compiler_params=_params(("arbitrary", "arbitrary"), 32),
        name="gdn_conv_prompt",
    )(qkv, st8, conv_w)


def gdn_conv_sample(qkv, row0, st8, conv_w, layer, n_rows, seq_len, key_dim):
    c_dim = qkv.shape[1]
    tc = 512
    tr = _pick(n_rows, 256)
    assert row0 % tr == 0
    rb0 = row0 // tr
    nqb, nqkb = key_dim // tc, 2 * key_dim // tc
    return pl.pallas_call(
        functools.partial(_conv_sample_kernel, seq_len=seq_len, n_q_blocks=nqb, n_qk_blocks=nqkb),
        out_shape=jax.ShapeDtypeStruct((n_rows, c_dim), BF16),
        grid=(n_rows // tr, c_dim // tc),
        in_specs=[pl.BlockSpec((tr, tc), lambda r, c: (rb0 + r, c)),
                  pl.BlockSpec((tr, tc), lambda r, c: (r, c)),
                  pl.BlockSpec((None, CONV_W, tc), lambda r, c: (layer, 0, c))],
        out_specs=pl.BlockSpec((tr, tc), lambda r, c: (r, c)),
        compiler_params=_params(("arbitrary", "arbitrary"), 32),
        name="gdn_conv_sample",
    )(qkv, st8, conv_w)


def _dot_nt(a, b):
    return lax.dot_general(a, b, (((1,), (1,)), ((), ())), preferred_element_type=F32)


def _dot_tn(a, b):
    return lax.dot_general(a, b, (((0,), (0,)), ((), ())), preferred_element_type=F32)


def _bdot(a, b):
    return jnp.dot(a.astype(BF16), b.astype(BF16), preferred_element_type=F32)


def _chunk_setup(kk, qkm, kf, qf, vf, gb, gt, h, n_heads, masks):
    incl, strict, incl_t, same = masks
    rows = kk.shape[0]
    lane_gb = lax.broadcasted_iota(I32, (rows, 2 * n_heads), 1)
    sub_gt = lax.broadcasted_iota(I32, (n_heads, rows), 0)
    beta = jnp.sum(jnp.where(lane_gb == h, gb, 0.0), axis=-1, keepdims=True)
    g_col = jnp.sum(jnp.where(lane_gb == h + n_heads, gb, 0.0), axis=-1, keepdims=True)
    g_row = jnp.sum(jnp.where(sub_gt == h, gt, 0.0), axis=0, keepdims=True)
    gcum = jnp.sum(jnp.where(incl, g_row, 0.0), axis=-1, keepdims=True)
    gcum_row = jnp.sum(jnp.where(incl_t, g_col, 0.0), axis=0, keepdims=True)
    decay = jnp.exp(jnp.where(incl, gcum - gcum_row, -jnp.inf))
    egc = jnp.exp(gcum)
    out = dict(low=jnp.where(strict, kk * decay * beta, 0.0),
               rhs=jnp.concatenate([vf * beta, kf * (beta * egc)], axis=1),
               qk=qkm * decay, q_g=qf * egc, gcum=gcum, kf=kf)
    if same is not None:
        out["gsum"] = jnp.sum(jnp.where(same, g_row, 0.0), axis=-1, keepdims=True)
    return out


def _unit_lower_inverse(lows, eye, n_sq):
    invs = [eye - low for low in lows]
    pows = lows
    for _ in range(n_sq):
        pows = [_bdot(p, p) for p in pows]
        invs = [inv + _bdot(inv, p) for inv, p in zip(invs, pows)]
    return invs


def _delta_kernel(q_ref, k_ref, v_ref, z_ref, gb_ref, gt_ref, nw_ref, o_init_ref, o_ref, s_ref, *,
                  n_chunks, n_heads, heads_per_qk, qk_per_step):
    c_len = CHUNK_B
    hpb = pl.program_id(1)
    hps = heads_per_qk
    n_local = qk_per_step * hps

    @pl.when(pl.program_id(2) == 0)
    def _():
        s_ref[...] = jnp.zeros_like(s_ref)

    ri = lax.broadcasted_iota(I32, (c_len, c_len), 0)
    ci = lax.broadcasted_iota(I32, (c_len, c_len), 1)
    masks = (ci <= ri, ci < ri, ri <= ci, None)
    eye = (ci == ri).astype(F32)
    nw = nw_ref[...]

    def prep_stages(chunks):
        st = {}

        def setup():
            items = {}
            for c in chunks:
                rs = slice(c * c_len, (c + 1) * c_len)
                gb = gb_ref[rs, :]
                gt = gt_ref[:, rs]
                for qh in range(qk_per_step):
                    kc = k_ref[rs, qh * DK:(qh + 1) * DK]
                    qc = q_ref[rs, qh * DK:(qh + 1) * DK]
                    kk = _dot_nt(kc, kc)
                    qkm = _dot_nt(qc, kc)
                    kf = kc.astype(F32)
                    qf = qc.astype(F32)
                    for hh in range(hps):
                        hl = qh * hps + hh
                        vf = v_ref[rs, hl * DV:(hl + 1) * DV].astype(F32)
                        items[(c, hl)] = _chunk_setup(kk, qkm, kf, qf, vf, gb, gt, hpb * n_local + hl,
                                                      n_heads, masks)
            st["prep"] = items
            st["pows"] = {key: p["low"] for key, p in items.items()}
            st["invs"] = {key: eye - p["low"] for key, p in items.items()}

        def square():
            st["pows"] = {key: _bdot(p, p) for key, p in st["pows"].items()}

        def update():
            st["invs"] = {key: inv + _bdot(inv, st["pows"][key]) for key, inv in st["invs"].items()}

        def solve():
            st["sols"] = {key: _bdot(inv, st["prep"][key]["rhs"]) for key, inv in st["invs"].items()}

        return st, [setup] + [square, update] * (c_len.bit_length() - 2) + [solve]

    state = [s_ref[0, hl] for hl in range(n_local)]

    def chain_stages(chunks, st):
        stages = []
        for c in chunks:
            tmp = {}

            def apply_state(c=c, tmp=tmp):
                tmp["ws"] = [_bdot(jnp.concatenate([st["sols"][(c, hl)][:, DV:], st["prep"][(c, hl)]["q_g"]],
                                                   axis=0), state[hl]) for hl in range(n_local)]

            def correct(c=c, tmp=tmp):
                tmp["u"] = [st["sols"][(c, hl)][:, :DV] - tmp["ws"][hl][:c_len] for hl in range(n_local)]
                tmp["out"] = [tmp["ws"][hl][c_len:] + _bdot(st["prep"][(c, hl)]["qk"], tmp["u"][hl])
                              for hl in range(n_local)]

            def advance(c=c, tmp=tmp):
                rs = slice(c * c_len, (c + 1) * c_len)
                for hl in range(n_local):
                    p = st["prep"][(c, hl)]
                    g_last = p["gcum"][c_len - 1:c_len, :]
                    k_tail = p["kf"] * jnp.exp(g_last - p["gcum"])
                    state[hl] = (state[hl] * jnp.exp(g_last)
                                 + _dot_tn(k_tail.astype(BF16), tmp["u"][hl].astype(BF16)))
                for hl in range(n_local):
                    hs = slice(hl * DV, (hl + 1) * DV)
                    zf = z_ref[rs, hs].astype(F32)
                    o_ref[rs, hs] = (_rms(tmp["out"][hl], nw) * _silu(zf)).astype(o_ref.dtype)

            stages += [apply_state, correct, advance]
        return stages

    half = max(n_chunks // 2, 1)
    st_a, prep_a = prep_stages(range(half))
    st_b, prep_b = prep_stages(range(half, n_chunks))
    for stage in prep_a:
        stage()
    chain_a = chain_stages(range(half), st_a)
    fill = iter(prep_b if half < n_chunks else [])
    per = -(-len(prep_b) // len(chain_a))
    for stage in chain_a:
        stage()
        for _ in range(per):
            nxt = next(fill, None)
            if nxt is not None:
                nxt()
    for nxt in fill:
        nxt()
    if half < n_chunks:
        for stage in chain_stages(range(half, n_chunks), st_b):
            stage()
    for hl in range(n_local):
        s_ref[0, hl] = state[hl]


def gated_delta_prompt(qkvc, z, gb, gt, norm_w, n_seq, seq_len, key_dim, n_heads):
    n_qk = key_dim // DK
    hps = n_heads // n_qk
    qps = 4 if n_qk % 4 == 0 else 1
    tc = _pick(seq_len, 256)
    nt = seq_len // tc
    qw, vw = qps * DK, qps * hps * DV
    k_blk0, v_blk0 = key_dim // qw, 2 * key_dim // vw
    rb = lambda b, t: b * nt + t
    return pl.pallas_call(
        functools.partial(_delta_kernel, n_chunks=tc // CHUNK_B, n_heads=n_heads, heads_per_qk=hps,
                          qk_per_step=qps),
        out_shape=(jax.ShapeDtypeStruct((z.shape[0], n_heads * DV), BF16),
                   jax.ShapeDtypeStruct((n_seq, n_heads, DK, DV), F32)),
        grid=(n_seq, n_qk // qps, nt),
        in_specs=[pl.BlockSpec((tc, qw), lambda b, h, t: (rb(b, t), h)),
                  pl.BlockSpec((tc, qw), lambda b, h, t: (rb(b, t), k_blk0 + h)),
                  pl.BlockSpec((tc, vw), lambda b, h, t: (rb(b, t), v_blk0 + h)),
                  pl.BlockSpec((tc, vw), lambda b, h, t: (rb(b, t), h)),
                  pl.BlockSpec((tc, 2 * n_heads), lambda b, h, t: (rb(b, t), 0)),
                  pl.BlockSpec((n_heads, tc), lambda b, h, t: (0, rb(b, t))),
                  pl.BlockSpec((1, DV), lambda b, h, t: (0, 0)),
                  pl.BlockSpec(memory_space=pl.ANY)],
        out_specs=(pl.BlockSpec((tc, vw), lambda b, h, t: (rb(b, t), h)),
                   pl.BlockSpec((1, qps * hps, DK, DV), lambda b, h, t: (b, h, 0, 0))),
        input_output_aliases={7: 0},
        compiler_params=_params(("arbitrary", "arbitrary", "arbitrary"), 32),
        name="gated_delta_prompt",
    )(qkvc, qkvc, qkvc, z, gb, gt, norm_w.reshape(1, DV), jnp.zeros((z.shape[0], n_heads * DV), BF16))


def _delta_sample_kernel(q_ref, k_ref, v_ref, z_ref, gb_ref, gt_ref, nw_ref, s0_ref, *rest,
                         seq_len, n_heads, heads_per_qk, slab):
    o_ref, s_ref = rest[-2:]
    if slab is not None:
        for other in range(s_ref.shape[0]):
            if other != slab:
                s_ref[other] = jnp.zeros(s_ref.shape[1:], F32)
        s_ref = s_ref.at[slab]
    c_len = CHUNK_B
    hpb = pl.program_id(1)
    hps = heads_per_qk
    rows = q_ref.shape[0]
    per = c_len // seq_len
    shift = seq_len.bit_length() - 1

    ri = lax.broadcasted_iota(I32, (c_len, c_len), 0)
    ci = lax.broadcasted_iota(I32, (c_len, c_len), 1)
    same = (ri >> shift) == (ci >> shift)
    masks = (same & (ci <= ri), same & (ci < ri), same & (ri <= ci), same)
    eye = (ci == ri).astype(F32)
    nw = nw_ref[...]

    prep = []
    for c in range(rows // c_len):
        rs = slice(c * c_len, (c + 1) * c_len)
        kc = k_ref[rs, :]
        qc = q_ref[rs, :]
        kk = _dot_nt(kc, kc)
        qkm = _dot_nt(qc, kc)
        kf = kc.astype(F32)
        qf = qc.astype(F32)
        gb = gb_ref[rs, :]
        gt = gt_ref[:, rs]
        for hh in range(hps):
            vf = v_ref[rs, hh * DV:(hh + 1) * DV].astype(F32)
            prep.append(_chunk_setup(kk, qkm, kf, qf, vf, gb, gt, hpb * hps + hh, n_heads, masks))
    invs = _unit_lower_inverse([p["low"] for p in prep], eye, seq_len.bit_length() - 2)
    sols = [_bdot(inv, p["rhs"]) for inv, p in zip(invs, prep)]

    for c in range(rows // c_len):
        rs = slice(c * c_len, (c + 1) * c_len)
        for hh in range(hps):
            p, sol = prep[c * hps + hh], sols[c * hps + hh]
            u_v, w_k = sol[:, :DV], sol[:, DV:]
            k_tail = p["kf"] * jnp.exp(p["gsum"] - p["gcum"])
            g_tail = jnp.exp(p["gsum"])
            s_old = [s0_ref[c * per + i, hh] for i in range(per)]
            seq = [slice(i * seq_len, (i + 1) * seq_len) for i in range(per)]
            ws = [_bdot(jnp.concatenate([w_k[r], p["q_g"][r]], axis=0), s) for r, s in zip(seq, s_old)]
            u_new = jnp.concatenate([u_v[r] - w[:seq_len] for r, w in zip(seq, ws)], axis=0)
            out = jnp.concatenate([w[seq_len:] for w in ws], axis=0) + _bdot(p["qk"], u_new)
            for i, r in enumerate(seq):
                s_ref[c * per + i, hh] = (s_old[i] * g_tail[i * seq_len:i * seq_len + 1, :]
                                          + _dot_tn(k_tail[r].astype(BF16), u_new[r].astype(BF16)))
            hs = slice(hh * DV, (hh + 1) * DV)
            zf = z_ref[rs, hs].astype(F32)
            o_ref[rs, hs] = (_rms(out, nw) * _silu(zf)).astype(o_ref.dtype)


def gated_delta_sample(qkvc, z, gb, gt, norm_w, s0, layer, row0, n_seq, seq_len, key_dim, o_all, s_all):
    n_heads = s0.shape[2]
    n_qk = key_dim // DK
    hps = n_heads // n_qk
    tr = LANES
    sps = tr // seq_len
    assert row0 % tr == 0 and n_seq % sps == 0 and seq_len == SUBLANES
    rb0 = row0 // tr
    vw = hps * DV
    v_blk0 = 2 * key_dim // vw
    args = [qkvc, qkvc, qkvc, z, gb, gt, norm_w.reshape(1, DV), s0, o_all]
    in_specs = [pl.BlockSpec((tr, DK), lambda g, h: (g, h)),
                pl.BlockSpec((tr, DK), lambda g, h: (g, n_qk + h)),
                pl.BlockSpec((tr, vw), lambda g, h: (g, v_blk0 + h)),
                pl.BlockSpec((tr, vw), lambda g, h: (rb0 + g, h)),
                pl.BlockSpec((tr, 2 * n_heads), lambda g, h: (rb0 + g, 0)),
                pl.BlockSpec((n_heads, tr), lambda g, h: (0, rb0 + g)),
                pl.BlockSpec((1, DV), lambda g, h: (0, 0)),
                pl.BlockSpec((None, sps, hps, DK, DV), lambda g, h: (layer, g, h, 0, 0)),
                pl.BlockSpec(memory_space=pl.ANY)]
    aliases = {8: 0}
    if s_all is not None:
        args.append(s_all)
        in_specs.append(pl.BlockSpec(memory_space=pl.ANY))
        aliases[9] = 1
        slab = None
        s_spec = pl.BlockSpec((None, sps, hps, DK, DV), lambda g, h: (layer, g, h, 0, 0))
    else:
        slab = layer
        s_spec = pl.BlockSpec((s0.shape[0], sps, hps, DK, DV), lambda g, h: (0, g, h, 0, 0))
    return pl.pallas_call(
        functools.partial(_delta_sample_kernel, seq_len=seq_len, n_heads=n_heads, heads_per_qk=hps, slab=slab),
        out_shape=(jax.ShapeDtypeStruct(o_all.shape, BF16), jax.ShapeDtypeStruct(s0.shape, F32)),
        grid=(n_seq // sps, n_qk),
        in_specs=in_specs,
        out_specs=(pl.BlockSpec((tr, vw), lambda g, h: (rb0 + g, h)), s_spec),
        input_output_aliases=aliases,
        compiler_params=_params(("arbitrary", "arbitrary"), 32),
        name="gated_delta_sample",
    )(*args)


def _router_kernel(x_ref, nw_ref, wr_ref, br_ref, h_ref, sel_ref, gate_ref, cnt_ref, cnt_sc):
    tm = x_ref.shape[0]

    @pl.when(pl.program_id(0) == 0)
    def _():
        cnt_sc[...] = jnp.zeros_like(cnt_sc)

    h = _rms(x_ref[...], nw_ref[...])
    h_ref[...] = h
    logits = jnp.dot(h, wr_ref[...], precision=lax.Precision.HIGHEST, preferred_element_type=F32) + br_ref[...]
    lane = lax.broadcasted_iota(I32, (tm, LANES), 1)
    lane_f = lane.astype(F32)
    big = float(LANES)

    is_g = lane < N_GROUPS
    gl = jnp.where(is_g, logits, -jnp.inf)
    ge = jnp.exp(gl - jnp.max(gl, axis=-1, keepdims=True))
    gp = ge / jnp.sum(ge, axis=-1, keepdims=True)
    p_group = jnp.max(gp, axis=-1, keepdims=True)
    g_sel = jnp.min(jnp.where(is_g & (gp == p_group), lane_f, big), axis=-1, keepdims=True)

    e_lane = lane - N_GROUPS
    in_grp = (e_lane >= 0) & (e_lane < N_EXPERTS) & ((e_lane >> 3).astype(F32) == g_sel)
    el = jnp.where(in_grp, logits, -jnp.inf)
    ee = jnp.exp(el - jnp.max(el, axis=-1, keepdims=True))
    ep = ee / jnp.sum(ee, axis=-1, keepdims=True)
    p1 = jnp.max(ep, axis=-1, keepdims=True)
    i1 = jnp.min(jnp.where(in_grp & (ep == p1), lane_f, big), axis=-1, keepdims=True)
    rest = in_grp & (lane_f != i1)
    p2 = jnp.max(jnp.where(rest, ep, -1.0), axis=-1, keepdims=True)
    i2 = jnp.min(jnp.where(rest & (ep == p2), lane_f, big), axis=-1, keepdims=True)
    denom = p1 + p2
    gate1 = p_group * p1 / denom
    gate2 = p_group * p2 / denom
    e1 = i1 - float(N_GROUPS)
    e2 = i2 - float(N_GROUPS)

    onehot = (lane_f == e1) | (lane_f == e2)
    oh = jnp.where(onehot, 1.0, 0.0)
    rr = lax.broadcasted_iota(I32, (tm, tm), 0)
    cc = lax.broadcasted_iota(I32, (tm, tm), 1)
    tri = jnp.where(cc < rr, 1.0, 0.0).astype(BF16)
    before = jnp.dot(tri, oh.astype(BF16), preferred_element_type=F32) + cnt_sc[...]
    r1 = jnp.sum(jnp.where(lane_f == e1, before, 0.0), axis=-1, keepdims=True)
    r2 = jnp.sum(jnp.where(lane_f == e2, before, 0.0), axis=-1, keepdims=True)
    cnt_sc[...] = cnt_sc[...] + jnp.sum(oh, axis=0, keepdims=True)
    cnt_ref[...] = cnt_sc[...]

    sel = jnp.where(lane == 0, e1, jnp.where(lane == 1, e2, jnp.where(lane == 2, r1, jnp.where(lane == 3, r2, 0.0))))
    sel_ref[...] = sel.astype(I32)
    gate_ref[...] = jnp.where(lane == 0, gate1, jnp.where(lane == 1, gate2, 0.0))


def moe_router(x, norm_w, w_route, b_route):
    n, d = x.shape
    tm = _pick(n, 256)
    return pl.pallas_call(
        _router_kernel,
        out_shape=(jax.ShapeDtypeStruct((n, d), F32),
                   jax.ShapeDtypeStruct((n, LANES), I32),
                   jax.ShapeDtypeStruct((n, LANES), F32),
                   jax.ShapeDtypeStruct((1, LANES), F32)),
        grid=(n // tm,),
        in_specs=[pl.BlockSpec((tm, d), lambda i: (i, 0)),
                  pl.BlockSpec((1, d), lambda i: (0, 0)),
                  pl.BlockSpec((d, LANES), lambda i: (0, 0)),
                  pl.BlockSpec((1, LANES), lambda i: (0, 0))],
        out_specs=(pl.BlockSpec((tm, d), lambda i: (i, 0)),
                   pl.BlockSpec((tm, LANES), lambda i: (i, 0)),
                   pl.BlockSpec((tm, LANES), lambda i: (i, 0)),
                   pl.BlockSpec((1, LANES), lambda i: (0, 0))),
        scratch_shapes=[pltpu.VMEM((1, LANES), F32)],
        compiler_params=_params(("arbitrary",), 32),
        name="moe_router",
    )(x, norm_w.reshape(1, d), w_route, b_route)


ISSUE_UNROLL = 8


def _slot(slot_ref, r, k):
    return slot_ref[0, 0, TOP_K * r + k]


def _row_copy_out(h_ref, xs_ref, sem, r, d):
    return pltpu.make_async_copy(h_ref.at[pl.ds(r, 1), :], xs_ref.at[pl.ds(d, 1), :], sem)


def _dispatch_kernel(slot_ref, h_ref, xs_in_ref, xs_ref, sem):
    del xs_in_ref
    tm = h_ref.shape[0]

    def issue(r, carry):
        for k in range(TOP_K):
            _row_copy_out(h_ref, xs_ref, sem, r, _slot(slot_ref, r, k)).start()
        return carry

    lax.fori_loop(0, tm, issue, 0, unroll=ISSUE_UNROLL)
    for k in range(TOP_K):
        pltpu.make_async_copy(h_ref, xs_ref.at[pl.ds(0, tm), :], sem).wait()


def moe_dispatch(h, slot3, n_slots):
    n, d = h.shape
    tm = slot3.shape[2] // TOP_K
    xs0 = jnp.zeros((n_slots, d), F32)
    return pl.pallas_call(
        _dispatch_kernel,
        out_shape=jax.ShapeDtypeStruct((n_slots, d), F32),
        grid=(n // tm,),
        in_specs=[pl.BlockSpec((1, 1, TOP_K * tm), lambda i: (i, 0, 0), memory_space=pltpu.SMEM),
                  pl.BlockSpec((tm, d), lambda i: (i, 0)),
                  pl.BlockSpec(memory_space=pl.ANY)],
        out_specs=pl.BlockSpec(memory_space=pl.ANY),
        scratch_shapes=[pltpu.SemaphoreType.DMA(())],
        input_output_aliases={2: 0},
        compiler_params=_params(("arbitrary",), 32),
        name="moe_dispatch",
    )(slot3, h, xs0)


def _ffn_kernel(be_ref, first_ref, nused_ref, x_ref, wg_ref, wu_ref, wd_ref, o_ref, wgb, wub, wdb):
    b = pl.program_id(0)
    valid = b < nused_ref[0]

    @pl.when(valid & (first_ref[b] == 1))
    def _():
        wgb[...] = wg_ref[...].astype(BF16)
        wub[...] = wu_ref[...].astype(BF16)
        wdb[...] = wd_ref[...].astype(BF16)

    @pl.when(valid)
    def _():
        x = x_ref[...].astype(BF16)
        gate = jnp.dot(x, wgb[...], preferred_element_type=F32)
        up = jnp.dot(x, wub[...], preferred_element_type=F32)
        hid = (_silu(gate) * up).astype(BF16)
        o_ref[...] = jnp.dot(hid, wdb[...], preferred_element_type=F32)

    @pl.when(jnp.logical_not(valid))
    def _():
        o_ref[...] = jnp.zeros_like(o_ref)


def moe_experts(xs, block_expert, block_first, n_used, w_gate, w_up, w_down, layer):
    n_slots, d = xs.shape
    de = w_gate.shape[-1]
    n_blocks = n_slots // MOE_ROWS
    xmap = lambda b, be, fi, nu: (jnp.minimum(b, jnp.maximum(nu[0] - 1, 0)), 0)
    grid_spec = pltpu.PrefetchScalarGridSpec(
        num_scalar_prefetch=3,
        grid=(n_blocks,),
        in_specs=[pl.BlockSpec((MOE_ROWS, d), xmap),
                  pl.BlockSpec((None, None, d, de), lambda b, be, fi, nu: (layer, be[b], 0, 0)),
                  pl.BlockSpec((None, None, d, de), lambda b, be, fi, nu: (layer, be[b], 0, 0)),
                  pl.BlockSpec((None, None, de, d), lambda b, be, fi, nu: (layer, be[b], 0, 0))],
        out_specs=pl.BlockSpec((MOE_ROWS, d), lambda b, be, fi, nu: (b, 0)),
        scratch_shapes=[pltpu.VMEM((d, de), BF16), pltpu.VMEM((d, de), BF16), pltpu.VMEM((de, d), BF16)],
    )
    return pl.pallas_call(
        _ffn_kernel,
        out_shape=jax.ShapeDtypeStruct((n_slots, d), F32),
        grid_spec=grid_spec,
        compiler_params=_params(("arbitrary",), 48),
        name="moe_experts",
    )(block_expert, block_first, n_used, xs, w_gate, w_up, w_down)


def _row_copy_in(yb_ref, buf_ref, sem, k, r, d):
    return pltpu.make_async_copy(yb_ref.at[pl.ds(d, 1), :], buf_ref.at[k, pl.ds(r, 1), :], sem)


def _combine_kernel(slot_ref, x_ref, gate_ref, nw_ref, yb_ref, o1_ref, o2_ref, buf_ref, sem, *, split_step):
    tm = x_ref.shape[0]

    def issue(r, carry):
        for k in range(TOP_K):
            _row_copy_in(yb_ref, buf_ref, sem, k, r, _slot(slot_ref, r, k)).start()
        return carry

    lax.fori_loop(0, tm, issue, 0, unroll=ISSUE_UNROLL)
    for k in range(TOP_K):
        pltpu.make_async_copy(yb_ref.at[pl.ds(0, tm), :], buf_ref.at[k], sem).wait()

    g = gate_ref[...]
    y = buf_ref[0] * g[:, 0:1] + buf_ref[1] * g[:, 1:2]
    xn = x_ref[...] + y
    hn = _rms(xn, nw_ref[...])
    if split_step is None:
        o1_ref[...] = xn
        o2_ref[...] = hn.astype(o2_ref.dtype)
    else:
        @pl.when(pl.program_id(0) < split_step)
        def _():
            o1_ref[...] = hn

        @pl.when(pl.program_id(0) >= split_step)
        def _():
            o2_ref[...] = hn


def moe_combine(x, yb, slot3, gates, next_norm_w, split_rows=None):
    n, d = x.shape
    tm = slot3.shape[2] // TOP_K
    if split_rows is None:
        split_step = None
        out_shape = (jax.ShapeDtypeStruct((n, d), F32), jax.ShapeDtypeStruct((n, d), BF16))
        out_specs = (pl.BlockSpec((tm, d), lambda i: (i, 0)), pl.BlockSpec((tm, d), lambda i: (i, 0)))
    else:
        assert split_rows % tm == 0 and 0 < split_rows < n
        split_step = split_rows // tm
        out_shape = (jax.ShapeDtypeStruct((split_rows, d), F32), jax.ShapeDtypeStruct((n - split_rows, d), F32))
        out_specs = (pl.BlockSpec((tm, d), lambda i: (jnp.minimum(i, split_step - 1), 0)),
                     pl.BlockSpec((tm, d), lambda i: (jnp.maximum(i - split_step, 0), 0)))
    return pl.pallas_call(
        functools.partial(_combine_kernel, split_step=split_step),
        out_shape=out_shape,
        grid=(n // tm,),
        in_specs=[pl.BlockSpec((1, 1, TOP_K * tm), lambda i: (i, 0, 0), memory_space=pltpu.SMEM),
                  pl.BlockSpec((tm, d), lambda i: (i, 0)),
                  pl.BlockSpec((tm, LANES), lambda i: (i, 0)),
                  pl.BlockSpec((1, d), lambda i: (0, 0)),
                  pl.BlockSpec(memory_space=pl.ANY)],
        out_specs=out_specs,
        scratch_shapes=[pltpu.VMEM((TOP_K, tm, d), F32), pltpu.SemaphoreType.DMA(())],
        compiler_params=_params(("arbitrary",), 40),
        name="moe_combine",
    )(slot3, x, gates, next_norm_w.reshape(1, d), yb)


def hier_moe(x, ffn_norm_w, w_route, b_route, w_gate, w_up, w_down, layer, next_norm_w, split_rows=None):
    n, _ = x.shape
    h, sel, gates, cnt = moe_router(x, ffn_norm_w, w_route, b_route)
    counts = cnt[0, :N_EXPERTS].astype(I32)
    padded = (counts + MOE_ROWS - 1) // MOE_ROWS * MOE_ROWS
    pend = jnp.cumsum(padded)
    pstart = (pend - padded).astype(I32)
    tm = _pick(n, 256)
    expert, rank = sel[:, :TOP_K], sel[:, TOP_K:2 * TOP_K]
    is_e = expert[:, :, None] == jnp.arange(N_EXPERTS, dtype=I32)
    slot3 = (rank + jnp.sum(jnp.where(is_e, pstart, 0), axis=-1)).reshape(n // tm, 1, TOP_K * tm)
    n_blocks = -(-(n * TOP_K) // MOE_ROWS) + N_EXPERTS
    n_used = pend[-1] // MOE_ROWS
    blk = jnp.arange(n_blocks, dtype=I32)
    last = jnp.maximum(n_used - 1, 0)
    bstart = jnp.minimum(blk, last) * MOE_ROWS
    block_expert = jnp.minimum(jnp.sum(pend[None, :] <= bstart[:, None], axis=1), N_EXPERTS - 1).astype(I32)
    block_first = (bstart == pstart[block_expert]).astype(I32)
    xs = moe_dispatch(h, slot3, n_blocks * MOE_ROWS)
    yb = moe_experts(xs, block_expert, block_first, n_used.reshape(1).astype(I32), w_gate, w_up, w_down, layer)
    return moe_combine(x, yb, slot3, gates, next_norm_w, split_rows)


def kernel(x_prompt, x_sample, state_delta_S, state_delta_conv, norm_mixer_w, norm_ffn_w, norm_final_w,
           a_w_in, a_ln_g, a_ln_b, a_w_s, a_b_s, a_w_out,
           b_w_in, b_conv_w, b_a_log, b_dt_bias, b_norm_w, b_w_out,
           moe_w_group, moe_b_group, moe_w_router, moe_b_router, moe_w_gate, moe_w_up, moe_w_down):
    nb, seq, d = x_prompt.shape
    db, dseq, _ = x_sample.shape
    depth = norm_mixer_w.shape[0]
    n_p, n_s = nb * seq, db * dseq
    n = n_p + n_s
    a_width = a_ln_g.shape[1]
    n_groups_a = a_width // A_GROUP_DIM
    n_heads = b_a_log.shape[1]
    value_dim = n_heads * DV
    conv_dim = b_conv_w.shape[2]
    key_dim = (conv_dim - value_dim) // 2
    assert seq % CHUNK_A == 0 and n_s % CHUNK_A == 0 and CHUNK_A % dseq == 0 and dseq == SUBLANES
    assert seq % CHUNK_B == 0 and dseq <= CHUNK_B and dseq >= CONV_W - 1


    causal = jnp.tril(jnp.ones((CHUNK_A, CHUNK_A), bool))
    ws_p = jnp.where(causal, a_w_s, 0.0)
    per = CHUNK_A // dseq
    ws_s = jnp.where(causal[:dseq, :dseq], a_w_s[:, :, :dseq, :dseq], 0.0)
    ws_s = jnp.einsum("ij,lgts->lgitjs", jnp.eye(per, dtype=F32), ws_s).reshape(a_w_s.shape)
    ws2 = jnp.stack([ws_p, ws_s], axis=1).astype(BF16)
    bias_p = jnp.repeat(jnp.swapaxes(a_b_s, 1, 2), A_GROUP_DIM, axis=2)
    bias_s = jnp.tile(bias_p[:, :dseq], (1, per, 1))
    bias2 = jnp.stack([bias_p, bias_s], axis=1)

    w_route = jnp.concatenate([moe_w_group, moe_w_router,
                               jnp.zeros((depth, d, LANES - N_GROUPS - N_EXPERTS), F32)], axis=2)
    b_route = jnp.concatenate([moe_b_group, moe_b_router,
                               jnp.zeros((depth, LANES - N_GROUPS - N_EXPERTS), F32)], axis=1)
    b_w_in_t = jnp.swapaxes(b_w_in, 1, 2)
    zeros_h = jnp.zeros_like(b_a_log)
    a_log2 = jnp.concatenate([zeros_h, b_a_log], axis=1)
    dt2 = jnp.concatenate([zeros_h, b_dt_bias], axis=1)

    chunk_v, s_prompt, conv_prompt, conv_sample = [], [], [], []
    s_sample = None
    x, h = concat_rmsnorm(x_prompt.reshape(n_p, d), x_sample.reshape(n_s, d), norm_mixer_w[0])
    for i in range(depth):
        j = i // 2
        if i % 2 == 0:
            u = matmul(h, a_w_in, j, 0, a_width, BF16, act="gelu")
            vpre = matmul(h, a_w_in, j, a_width, a_width, F32, act="gelu")
            gated, v_s = gmlp_gate(u, vpre, a_ln_g[j], a_ln_b[j], ws2[j], bias2[j], n_p // CHUNK_A)
            chunk_v.append(v_s.reshape(db, dseq, a_width))
            x = matmul(gated, a_w_out, j, 0, d, F32, res=x)
        else:
            qkv = matmul(h, b_w_in_t, j, 0, conv_dim, F32, w_t=True)
            z = matmul(h, b_w_in_t, j, conv_dim, value_dim, BF16, w_t=True)
            gb = matmul(h, b_w_in_t, j, conv_dim + value_dim, 2 * n_heads, F32, act="gdn_gates",
                        extra=(a_log2[j:j + 1], dt2[j:j + 1]), w_t=True)
            gt = gb[:, n_heads:].T
            keep = CONV_W - 1
            conv_prompt.append(jnp.stack([qkv[(b + 1) * seq - keep:(b + 1) * seq] for b in range(nb)]))
            conv_sample.append(qkv[n_p:].reshape(db, dseq, conv_dim)[:, dseq - keep:])
            pad8 = ((0, 0), (SUBLANES - keep, 0), (0, 0))
            st8_p = jnp.zeros((nb * SUBLANES, conv_dim), F32)
            st8_s = jnp.pad(state_delta_conv[j], pad8).reshape(db * SUBLANES, conv_dim)
            qc_p = gdn_conv_prompt(qkv, st8_p, b_conv_w, j, nb, seq, key_dim)
            qc_s = gdn_conv_sample(qkv, n_p, st8_s, b_conv_w, j, n_s, dseq, key_dim)
            o_p, s_p = gated_delta_prompt(qc_p, z, gb, gt, b_norm_w[j], nb, seq, key_dim, n_heads)
            gated, s_sample = gated_delta_sample(qc_s, z, gb, gt, b_norm_w[j], state_delta_S, j, n_p, db, dseq,
                                                 key_dim, o_p, s_sample)
            s_prompt.append(s_p)
            x = matmul(gated, b_w_out, j, 0, d, F32, res=x)
        moe_w = (norm_ffn_w[i], w_route[i], b_route[i:i + 1], moe_w_gate, moe_w_up, moe_w_down, i)
        if i < depth - 1:
            x, h = hier_moe(x, *moe_w, norm_mixer_w[i + 1])
        else:
            y_p, y_s = hier_moe(x, *moe_w, norm_final_w, split_rows=n_p)

    y_prompt = y_p.reshape(nb, seq, d)
    y_sample = y_s.reshape(db, dseq, d)
    return (y_prompt, y_sample, jnp.stack(chunk_v), jnp.stack(s_prompt), jnp.stack(conv_prompt),
            s_sample, jnp.stack(conv_sample))
```

```python
import functools
import math

import jax
import jax.numpy as jnp
from jax import lax
from jax.experimental import pallas as pl
from jax.experimental.pallas import tpu as pltpu

F32 = jnp.float32
BF16 = jnp.bfloat16
I32 = jnp.int32

EPS = 1e-6
LANES = 128
SUBLANES = 8
CHUNK_A = 128
A_GROUP_DIM = 128
DK = 128
DV = 128
CONV_W = 4
CHUNK_B = 64
N_GROUPS = 8
EXPERTS_PER_GROUP = 8
N_EXPERTS = N_GROUPS * EXPERTS_PER_GROUP
TOP_K = 2
MOE_ROWS = 256
MIB = 2 ** 20


def _params(semantics, vmem_mib):
    return pltpu.CompilerParams(dimension_semantics=semantics, vmem_limit_bytes=vmem_mib * MIB)


def _pick(n, pref):
    t = min(n, pref)
    while n % t:
        t -= SUBLANES
    return t


def _rms(x, w):
    return x * lax.rsqrt(jnp.mean(x * x, axis=-1, keepdims=True) + EPS) * w


def _silu(x):
    return x * jax.nn.sigmoid(x)


def _concat_rmsnorm_kernel(xa_ref, xb_ref, w_ref, x_ref, h_ref, *, steps_a):
    @pl.when(pl.program_id(0) < steps_a)
    def _():
        x_ref[...] = xa_ref[...]

    @pl.when(pl.program_id(0) >= steps_a)
    def _():
        x_ref[...] = xb_ref[...]

    h_ref[...] = _rms(x_ref[...], w_ref[...]).astype(h_ref.dtype)


def concat_rmsnorm(xa, xb, w):
    (na, d), (nb_, _) = xa.shape, xb.shape
    tm = _pick(math.gcd(na, nb_), 512)
    steps_a = na // tm
    return pl.pallas_call(
        functools.partial(_concat_rmsnorm_kernel, steps_a=steps_a),
        out_shape=(jax.ShapeDtypeStruct((na + nb_, d), F32), jax.ShapeDtypeStruct((na + nb_, d), BF16)),
        grid=((na + nb_) // tm,),
        in_specs=[pl.BlockSpec((tm, d), lambda i: (jnp.minimum(i, steps_a - 1), 0)),
                  pl.BlockSpec((tm, d), lambda i: (jnp.maximum(i - steps_a, 0), 0)),
                  pl.BlockSpec((1, d), lambda i: (0, 0))],
        out_specs=(pl.BlockSpec((tm, d), lambda i: (i, 0)), pl.BlockSpec((tm, d), lambda i: (i, 0))),
        compiler_params=_params(("arbitrary",), 32),
        name="concat_rmsnorm",
    )(xa, xb, w.reshape(1, d))


def _mm_kernel(*refs, act, has_res, n_extra, w_t):
    a_ref, w_ref = refs[:2]
    extra = refs[2:2 + n_extra]
    rest = refs[2 + n_extra:]
    if has_res:
        r_ref, o_ref, wb_ref = rest
    else:
        o_ref, wb_ref = rest

    @pl.when(pl.program_id(1) == 0)
    def _():
        wb_ref[...] = w_ref[...].astype(BF16)

    if w_t:
        acc = _dot_nt(a_ref[...], wb_ref[...])
    else:
        acc = jnp.dot(a_ref[...], wb_ref[...], preferred_element_type=F32)
    if act == "gelu":
        acc = jax.nn.gelu(acc)
    elif act == "gdn_gates":
        a_log_ref, dt_ref = extra
        h = acc.shape[1] // 2
        lane = lax.broadcasted_iota(I32, acc.shape, 1)
        g = -jnp.exp(a_log_ref[...]) * jax.nn.softplus(acc + dt_ref[...])
        acc = jnp.where(lane < h, jax.nn.sigmoid(acc), g)
    if has_res:
        acc = acc + r_ref[...]
    o_ref[...] = acc.astype(o_ref.dtype)


def matmul(a, w, layer, col0, ncols, out_dtype, *, act=None, res=None, extra=(), tm=1024, tn=512, w_t=False):
    m, k = a.shape
    tm = _pick(m, tm)
    if k <= 2048 and ncols % (2 * tn) == 0 and col0 % (2 * tn) == 0:
        tn = 2 * tn
    tn = min(tn, ncols)
    assert ncols % tn == 0 and col0 % tn == 0
    c0 = col0 // tn
    if w_t:
        w_spec = pl.BlockSpec((None, tn, k), lambda j, i: (layer, c0 + j, 0))
    else:
        w_spec = pl.BlockSpec((None, k, tn), lambda j, i: (layer, 0, c0 + j))
    in_specs = [pl.BlockSpec((tm, k), lambda j, i: (i, 0)), w_spec]
    args = [a, w]
    for e in extra:
        in_specs.append(pl.BlockSpec((1, tn), lambda j, i: (0, j)))
        args.append(e)
    if res is not None:
        in_specs.append(pl.BlockSpec((tm, tn), lambda j, i: (i, j)))
        args.append(res)
    out_bytes = jnp.dtype(out_dtype).itemsize
    vmem = (2 * tm * k * 2 + 2 * k * tn * 4 + k * tn * 2 + 2 * tm * tn * out_bytes
            + (2 * tm * tn * 4 if res is not None else 0) + 3 * tm * tn * 4)
    return pl.pallas_call(
        functools.partial(_mm_kernel, act=act, has_res=res is not None, n_extra=len(extra), w_t=w_t),
        out_shape=jax.ShapeDtypeStruct((m, ncols), out_dtype),
        grid=(ncols // tn, m // tm),
        in_specs=in_specs,
        out_specs=pl.BlockSpec((tm, tn), lambda j, i: (i, j)),
        scratch_shapes=[pltpu.VMEM((tn, k) if w_t else (k, tn), BF16)],
        compiler_params=_params(("arbitrary", "arbitrary"), min(56, vmem // MIB + 8)),
        name="matmul_" + (act or "plain") + ("_res" if res is not None else ""),
    )(*args)


def _gmlp_gate_kernel(u_ref, v_ref, g_ref, b_ref, ws_ref, bias_ref, o_ref, vo_ref, *, n_groups, first_sample):
    v = v_ref[...]
    xc = v - jnp.mean(v, axis=-1, keepdims=True)
    var = jnp.mean(xc * xc, axis=-1, keepdims=True)
    vn = xc * lax.rsqrt(var + EPS) * g_ref[...] + b_ref[...]

    @pl.when(pl.program_id(0) >= first_sample)
    def _():
        vo_ref[...] = vn

    vb = vn.astype(BF16)
    for g in range(n_groups):
        sl = slice(g * A_GROUP_DIM, (g + 1) * A_GROUP_DIM)
        s = jnp.dot(ws_ref[g], vb[:, sl], preferred_element_type=F32) + bias_ref[:, sl]
        o_ref[:, sl] = (u_ref[:, sl].astype(F32) * s).astype(BF16)


def gmlp_gate(u, vpre, ln_g, ln_b, ws2, bias2, n_prompt_chunks):
    n, aw = u.shape
    n_chunks = n // CHUNK_A
    n_groups = aw // A_GROUP_DIM
    n_sample_rows = n - n_prompt_chunks * CHUNK_A
    which = lambda c: jnp.where(c >= n_prompt_chunks, 1, 0)
    return pl.pallas_call(
        functools.partial(_gmlp_gate_kernel, n_groups=n_groups, first_sample=n_prompt_chunks),
        out_shape=(jax.ShapeDtypeStruct((n, aw), BF16),
                   jax.ShapeDtypeStruct((n_sample_rows, aw), F32)),
        grid=(n_chunks,),
        in_specs=[pl.BlockSpec((CHUNK_A, aw), lambda c: (c, 0)),
                  pl.BlockSpec((CHUNK_A, aw), lambda c: (c, 0)),
                  pl.BlockSpec((1, aw), lambda c: (0, 0)),
                  pl.BlockSpec((1, aw), lambda c: (0, 0)),
                  pl.BlockSpec((None, n_groups, CHUNK_A, CHUNK_A), lambda c: (which(c), 0, 0, 0)),
                  pl.BlockSpec((None, CHUNK_A, aw), lambda c: (which(c), 0, 0))],
        out_specs=(pl.BlockSpec((CHUNK_A, aw), lambda c: (c, 0)),
                   pl.BlockSpec((CHUNK_A, aw), lambda c: (jnp.maximum(c - n_prompt_chunks, 0), 0))),
        compiler_params=_params(("arbitrary",), 40),
        name="gmlp_gate",
    )(u, vpre, ln_g.reshape(1, aw), ln_b.reshape(1, aw), ws2, bias2)


def _conv_epilogue(y, o_ref, rows, n_q_blocks, n_qk_blocks):
    c = pl.program_id(1)
    y = _silu(y)
    tc = y.shape[1]

    @pl.when(c < n_qk_blocks)
    def _():
        scale = jnp.where(c < n_q_blocks, DK ** -0.5, 1.0).astype(F32)
        for j in range(tc // DK):
            blk = y[:, j * DK:(j + 1) * DK]
            nrm = blk * lax.rsqrt(jnp.sum(blk * blk, axis=-1, keepdims=True) + EPS)
            o_ref[rows, j * DK:(j + 1) * DK] = (nrm * scale).astype(o_ref.dtype)

    @pl.when(c >= n_qk_blocks)
    def _():
        o_ref[rows, :] = y.astype(o_ref.dtype)


def _conv_prompt_kernel(x_ref, st_ref, w_ref, o_ref, *, sub, n_q_blocks, n_qk_blocks):
    t_len, tc = x_ref.shape
    w = w_ref[...]
    r8 = lax.broadcasted_iota(I32, (SUBLANES, tc), 0)

    def body(i, carry):
        r0 = pl.multiple_of(i * sub, sub)
        cur = x_ref[pl.ds(r0, sub), :]
        p0 = pl.multiple_of(jnp.maximum(r0 - SUBLANES, 0), SUBLANES)
        prev8 = jnp.where(i == 0, st_ref[...], x_ref[pl.ds(p0, SUBLANES), :])
        acc = cur * w[CONV_W - 1:CONV_W, :]
        for k in range(1, CONV_W):
            xs = pltpu.roll(cur, k, 0)
            top = jnp.where(r8 < k, pltpu.roll(prev8, k, 0), xs[:SUBLANES])
            if sub > SUBLANES:
                xs = jnp.concatenate([top, xs[SUBLANES:]], axis=0)
            else:
                xs = top
            acc = acc + xs * w[CONV_W - 1 - k:CONV_W - k, :]
        _conv_epilogue(acc, o_ref, pl.ds(r0, sub), n_q_blocks, n_qk_blocks)
        return carry

    lax.fori_loop(0, t_len // sub, body, 0)


def _conv_sample_kernel(x_ref, st_ref, w_ref, o_ref, *, seq_len, n_q_blocks, n_qk_blocks):
    rows, tc = x_ref.shape
    assert seq_len == SUBLANES
    w = w_ref[...]
    x = x_ref[...]
    st = st_ref[...]
    t = lax.broadcasted_iota(I32, (rows, tc), 0) & (seq_len - 1)
    acc = x * w[CONV_W - 1:CONV_W, :]
    for k in range(1, CONV_W):
        xs = pltpu.roll(x, k, 0)
        ss = pltpu.roll(st, rows - (seq_len - k), 0)
        acc = acc + jnp.where(t >= k, xs, ss) * w[CONV_W - 1 - k:CONV_W - k, :]
    _conv_epilogue(acc, o_ref, slice(None), n_q_blocks, n_qk_blocks)


def gdn_conv_prompt(qkv, st8, conv_w, layer, n_seq, seq_len, key_dim):
    c_dim = qkv.shape[1]
    tc = 256
    sub = _pick(seq_len, 128)
    nqb, nqkb = key_dim // tc, 2 * key_dim // tc
    return pl.pallas_call(
        functools.partial(_conv_prompt_kernel, sub=sub, n_q_blocks=nqb, n_qk_blocks=nqkb),
        out_shape=jax.ShapeDtypeStruct((n_seq * seq_len, c_dim), BF16),
        grid=(n_seq, c_dim // tc),
        in_specs=[pl.BlockSpec((seq_len, tc), lambda b, c: (b, c)),
                  pl.BlockSpec((SUBLANES, tc), lambda b, c: (b, c)),
                  pl.BlockSpec((None, CONV_W, tc), lambda b, c: (layer, 0, c))],
        out_specs=pl.BlockSpec((seq_len, tc), lambda b, c: (b, c)),
        compiler_params=_params(("arbitrary", "arbitrary"), 32),
        name="gdn_conv_prompt",
    )(qkv, st8, conv_w)


def gdn_conv_sample(qkv, row0, st8, conv_w, layer, n_rows, seq_len, key_dim):
    c_dim = qkv.shape[1]
    tc = 512
    tr = _pick(n_rows, 256)
    assert row0 % tr == 0
    rb0 = row0 // tr
    nqb, nqkb = key_dim // tc, 2 * key_dim // tc
    return pl.pallas_call(
        functools.partial(_conv_sample_kernel, seq_len=seq_len, n_q_blocks=nqb, n_qk_blocks=nqkb),
        out_shape=jax.ShapeDtypeStruct((n_rows, c_dim), BF16),
        grid=(n_rows // tr, c_dim // tc),
        in_specs=[pl.BlockSpec((tr, tc), lambda r, c: (rb0 + r, c)),
                  pl.BlockSpec((tr, tc), lambda r, c: (r, c)),
                  pl.BlockSpec((None, CONV_W, tc), lambda r, c: (layer, 0, c))],
        out_specs=pl.BlockSpec((tr, tc), lambda r, c: (r, c)),
        compiler_params=_params(("arbitrary", "arbitrary"), 32),
        name="gdn_conv_sample",
    )(qkv, st8, conv_w)


def _dot_nt(a, b):
    return lax.dot_general(a, b, (((1,), (1,)), ((), ())), preferred_element_type=F32)


def _dot_tn(a, b):
    return lax.dot_general(a, b, (((0,), (0,)), ((), ())), preferred_element_type=F32)


def _bdot(a, b):
    return jnp.dot(a.astype(BF16), b.astype(BF16), preferred_element_type=F32)


def _chunk_masks(c_len, seq_len=None):
    ri = lax.broadcasted_iota(I32, (c_len, 2 * c_len), 0)
    ci = lax.broadcasted_iota(I32, (c_len, 2 * c_len), 1)
    left = ci < c_len
    if seq_len is not None:
        shift = seq_len.bit_length() - 1
        left = left & ((ri >> shift) == (ci >> shift))
    sq = lambda m: m[:, :c_len]
    return dict(incl=sq(left & (ci <= ri)), incl_w=left & (ci <= ri), strict_w=left & (ci < ri),
                incl_t_w=left & (ri <= ci), eye_w=jnp.where(ci == ri + c_len, 1.0, 0.0),
                left_w=ci < c_len, same=sq(left) if seq_len is not None else None)


def _chunk_setup(kk_w, qkm, kf, qf, vf, gb, gt, h, n_heads, masks):
    rows = qkm.shape[0]
    lane_gb = lax.broadcasted_iota(I32, (rows, 2 * n_heads), 1)
    sub_gt = lax.broadcasted_iota(I32, (n_heads, rows), 0)
    beta = jnp.sum(jnp.where(lane_gb == h, gb, 0.0), axis=-1, keepdims=True)
    g_col = jnp.sum(jnp.where(lane_gb == h + n_heads, gb, 0.0), axis=-1, keepdims=True)
    g_row = jnp.sum(jnp.where(sub_gt == h, gt, 0.0), axis=0, keepdims=True)
    gcum = jnp.sum(jnp.where(masks["incl"], g_row, 0.0), axis=-1, keepdims=True)
    gcum_row = jnp.sum(jnp.where(masks["incl_t_w"], g_col, 0.0), axis=0, keepdims=True)
    decay_w = jnp.exp(jnp.where(masks["incl_w"], gcum - gcum_row, -jnp.inf))
    egc = jnp.exp(gcum)
    out = dict(w0=masks["eye_w"] - jnp.where(masks["strict_w"], kk_w * decay_w * beta, 0.0),
               rhs=jnp.concatenate([vf * beta, kf * (beta * egc)], axis=1),
               qk=qkm * decay_w[:, :rows], q_g=qf * egc, gcum=gcum, kf=kf)
    if masks["same"] is not None:
        out["gsum"] = jnp.sum(jnp.where(masks["same"], g_row, 0.0), axis=-1, keepdims=True)
    return out


def _product_step(w, left_w):
    wb = w.astype(BF16)
    prod = jnp.dot(wb, jnp.concatenate([wb, jnp.zeros_like(wb)], axis=0), preferred_element_type=F32)
    return jnp.where(left_w, prod, w + prod)


def _apply_inverse(w, rhs):
    rb = rhs.astype(BF16)
    return jnp.dot(w.astype(BF16), jnp.concatenate([jnp.zeros_like(rb), rb], axis=0), preferred_element_type=F32)


def _delta_kernel(q_ref, k_ref, v_ref, z_ref, gb_ref, gt_ref, nw_ref, o_init_ref, o_ref, s_ref, *,
                  n_chunks, n_heads, heads_per_qk, qk_per_step):
    c_len = CHUNK_B
    hpb = pl.program_id(1)
    hps = heads_per_qk
    n_local = qk_per_step * hps

    @pl.when(pl.program_id(2) == 0)
    def _():
        s_ref[...] = jnp.zeros_like(s_ref)

    masks = _chunk_masks(c_len)
    nw = nw_ref[...]

    def prep_stages(chunks):
        st = {}

        def setup():
            items = {}
            for c in chunks:
                rs = slice(c * c_len, (c + 1) * c_len)
                gb = gb_ref[rs, :]
                gt = gt_ref[:, rs]
                for qh in range(qk_per_step):
                    kc = k_ref[rs, qh * DK:(qh + 1) * DK]
                    qc = q_ref[rs, qh * DK:(qh + 1) * DK]
                    kk_w = _dot_nt(kc, jnp.concatenate([kc, jnp.zeros_like(kc)], axis=0))
                    qkm = _dot_nt(qc, kc)
                    kf = kc.astype(F32)
                    qf = qc.astype(F32)
                    for hh in range(hps):
                        hl = qh * hps + hh
                        vf = v_ref[rs, hl * DV:(hl + 1) * DV].astype(F32)
                        items[(c, hl)] = _chunk_setup(kk_w, qkm, kf, qf, vf, gb, gt, hpb * n_local + hl,
                                                      n_heads, masks)
            st["prep"] = items
            st["w"] = {key: p["w0"] for key, p in items.items()}

        def step():
            st["w"] = {key: _product_step(w, masks["left_w"]) for key, w in st["w"].items()}

        def solve():
            st["sols"] = {key: _apply_inverse(w, st["prep"][key]["rhs"]) for key, w in st["w"].items()}

        return st, [setup] + [step] * (c_len.bit_length() - 1) + [solve]

    state = [s_ref[0, hl] for hl in range(n_local)]

    def chain_stages(chunks, st):
        stages = []
        for c in chunks:
            tmp = {}

            def apply_state(c=c, tmp=tmp):
                tmp["ws"] = [_bdot(jnp.concatenate([st["sols"][(c, hl)][:, DV:], st["prep"][(c, hl)]["q_g"]],
                                                   axis=0), state[hl]) for hl in range(n_local)]

            def correct(c=c, tmp=tmp):
                tmp["u"] = [st["sols"][(c, hl)][:, :DV] - tmp["ws"][hl][:c_len] for hl in range(n_local)]
                tmp["out"] = [tmp["ws"][hl][c_len:] + _bdot(st["prep"][(c, hl)]["qk"], tmp["u"][hl])
                              for hl in range(n_local)]

            def advance(c=c, tmp=tmp):
                rs = slice(c * c_len, (c + 1) * c_len)
                for hl in range(n_local):
                    p = st["prep"][(c, hl)]
                    g_last = p["gcum"][c_len - 1:c_len, :]
                    k_tail = p["kf"] * jnp.exp(g_last - p["gcum"])
                    state[hl] = (state[hl] * jnp.exp(g_last)
                                 + _dot_tn(k_tail.astype(BF16), tmp["u"][hl].astype(BF16)))
                for hl in range(n_local):
                    hs = slice(hl * DV, (hl + 1) * DV)
                    zf = z_ref[rs, hs].astype(F32)
                    o_ref[rs, hs] = (_rms(tmp["out"][hl], nw) * _silu(zf)).astype(o_ref.dtype)

            stages += [apply_state, correct, advance]
        return stages

    half = max(n_chunks // 2, 1)
    st_a, prep_a = prep_stages(range(half))
    st_b, prep_b = prep_stages(range(half, n_chunks))
    for stage in prep_a:
        stage()
    chain_a = chain_stages(range(half), st_a)
    fill = iter(prep_b if half < n_chunks else [])
    per = -(-len(prep_b) // len(chain_a))
    for stage in chain_a:
        stage()
        for _ in range(per):
            nxt = next(fill, None)
            if nxt is not None:
                nxt()
    for nxt in fill:
        nxt()
    if half < n_chunks:
        for stage in chain_stages(range(half, n_chunks), st_b):
            stage()
    for hl in range(n_local):
        s_ref[0, hl] = state[hl]


def gated_delta_prompt(qkvc, z, gb, gt, norm_w, n_seq, seq_len, key_dim, n_heads):
    n_qk = key_dim // DK
    hps = n_heads // n_qk
    qps = 4 if n_qk % 4 == 0 else 1
    tc = _pick(seq_len, 256)
    nt = seq_len // tc
    qw, vw = qps * DK, qps * hps * DV
    k_blk0, v_blk0 = key_dim // qw, 2 * key_dim // vw
    rb = lambda b, t: b * nt + t
    return pl.pallas_call(
        functools.partial(_delta_kernel, n_chunks=tc // CHUNK_B, n_heads=n_heads, heads_per_qk=hps,
                          qk_per_step=qps),
        out_shape=(jax.ShapeDtypeStruct((z.shape[0], n_heads * DV), BF16),
                   jax.ShapeDtypeStruct((n_seq, n_heads, DK, DV), F32)),
        grid=(n_seq, n_qk // qps, nt),
        in_specs=[pl.BlockSpec((tc, qw), lambda b, h, t: (rb(b, t), h)),
                  pl.BlockSpec((tc, qw), lambda b, h, t: (rb(b, t), k_blk0 + h)),
                  pl.BlockSpec((tc, vw), lambda b, h, t: (rb(b, t), v_blk0 + h)),
                  pl.BlockSpec((tc, vw), lambda b, h, t: (rb(b, t), h)),
                  pl.BlockSpec((tc, 2 * n_heads), lambda b, h, t: (rb(b, t), 0)),
                  pl.BlockSpec((n_heads, tc), lambda b, h, t: (0, rb(b, t))),
                  pl.BlockSpec((1, DV), lambda b, h, t: (0, 0)),
                  pl.BlockSpec(memory_space=pl.ANY)],
        out_specs=(pl.BlockSpec((tc, vw), lambda b, h, t: (rb(b, t), h)),
                   pl.BlockSpec((1, qps * hps, DK, DV), lambda b, h, t: (b, h, 0, 0))),
        input_output_aliases={7: 0},
        compiler_params=_params(("arbitrary", "arbitrary", "arbitrary"), 32),
        name="gated_delta_prompt",
    )(qkvc, qkvc, qkvc, z, gb, gt, norm_w.reshape(1, DV), jnp.zeros((z.shape[0], n_heads * DV), BF16))


def _delta_sample_kernel(q_ref, k_ref, v_ref, z_ref, gb_ref, gt_ref, nw_ref, s0_ref, *rest,
                         seq_len, n_heads, heads_per_qk, slab):
    o_ref, s_ref = rest[-2:]
    if slab is not None:
        for other in range(s_ref.shape[0]):
            if other != slab:
                s_ref[other] = jnp.zeros(s_ref.shape[1:], F32)
        s_ref = s_ref.at[slab]
    c_len = CHUNK_B
    hpb = pl.program_id(1)
    hps = heads_per_qk
    rows = q_ref.shape[0]
    per = c_len // seq_len
    masks = _chunk_masks(c_len, seq_len)
    nw = nw_ref[...]

    prep = []
    for c in range(rows // c_len):
        rs = slice(c * c_len, (c + 1) * c_len)
        kc = k_ref[rs, :]
        qc = q_ref[rs, :]
        kk_w = _dot_nt(kc, jnp.concatenate([kc, jnp.zeros_like(kc)], axis=0))
        qkm = _dot_nt(qc, kc)
        kf = kc.astype(F32)
        qf = qc.astype(F32)
        gb = gb_ref[rs, :]
        gt = gt_ref[:, rs]
        for hh in range(hps):
            vf = v_ref[rs, hh * DV:(hh + 1) * DV].astype(F32)
            prep.append(_chunk_setup(kk_w, qkm, kf, qf, vf, gb, gt, hpb * hps + hh, n_heads, masks))
    ws_ = [p["w0"] for p in prep]
    for _ in range(seq_len.bit_length() - 1):
        ws_ = [_product_step(w, masks["left_w"]) for w in ws_]
    sols = [_apply_inverse(w, p["rhs"]) for w, p in zip(ws_, prep)]

    for c in range(rows // c_len):
        rs = slice(c * c_len, (c + 1) * c_len)
        for hh in range(hps):
            p, sol = prep[c * hps + hh], sols[c * hps + hh]
            u_v, w_k = sol[:, :DV], sol[:, DV:]
            k_tail = p["kf"] * jnp.exp(p["gsum"] - p["gcum"])
            g_tail = jnp.exp(p["gsum"])
            s_old = [s0_ref[c * per + i, hh] for i in range(per)]
            seq = [slice(i * seq_len, (i + 1) * seq_len) for i in range(per)]
            ws = [_bdot(jnp.concatenate([w_k[r], p["q_g"][r]], axis=0), s) for r, s in zip(seq, s_old)]
            u_new = jnp.concatenate([u_v[r] - w[:seq_len] for r, w in zip(seq, ws)], axis=0)
            out = jnp.concatenate([w[seq_len:] for w in ws], axis=0) + _bdot(p["qk"], u_new)
            for i, r in enumerate(seq):
                s_ref[c * per + i, hh] = (s_old[i] * g_tail[i * seq_len:i * seq_len + 1, :]
                                          + _dot_tn(k_tail[r].astype(BF16), u_new[r].astype(BF16)))
            hs = slice(hh * DV, (hh + 1) * DV)
            zf = z_ref[rs, hs].astype(F32)
            o_ref[rs, hs] = (_rms(out, nw) * _silu(zf)).astype(o_ref.dtype)


def gated_delta_sample(qkvc, z, gb, gt, norm_w, s0, layer, row0, n_seq, seq_len, key_dim, o_all, s_all):
    n_heads = s0.shape[2]
    n_qk = key_dim // DK
    hps = n_heads // n_qk
    tr = LANES
    sps = tr // seq_len
    assert row0 % tr == 0 and n_seq % sps == 0 and seq_len == SUBLANES
    rb0 = row0 // tr
    vw = hps * DV
    v_blk0 = 2 * key_dim // vw
    args = [qkvc, qkvc, qkvc, z, gb, gt, norm_w.reshape(1, DV), s0, o_all]
    in_specs = [pl.BlockSpec((tr, DK), lambda g, h: (g, h)),
                pl.BlockSpec((tr, DK), lambda g, h: (g, n_qk + h)),
                pl.BlockSpec((tr, vw), lambda g, h: (g, v_blk0 + h)),
                pl.BlockSpec((tr, vw), lambda g, h: (rb0 + g, h)),
                pl.BlockSpec((tr, 2 * n_heads), lambda g, h: (rb0 + g, 0)),
                pl.BlockSpec((n_heads, tr), lambda g, h: (0, rb0 + g)),
                pl.BlockSpec((1, DV), lambda g, h: (0, 0)),
                pl.BlockSpec((None, sps, hps, DK, DV), lambda g, h: (layer, g, h, 0, 0)),
                pl.BlockSpec(memory_space=pl.ANY)]
    aliases = {8: 0}
    if s_all is not None:
        args.append(s_all)
        in_specs.append(pl.BlockSpec(memory_space=pl.ANY))
        aliases[9] = 1
        slab = None
        s_spec = pl.BlockSpec((None, sps, hps, DK, DV), lambda g, h: (layer, g, h, 0, 0))
    else:
        slab = layer
        s_spec = pl.BlockSpec((s0.shape[0], sps, hps, DK, DV), lambda g, h: (0, g, h, 0, 0))
    return pl.pallas_call(
        functools.partial(_delta_sample_kernel, seq_len=seq_len, n_heads=n_heads, heads_per_qk=hps, slab=slab),
        out_shape=(jax.ShapeDtypeStruct(o_all.shape, BF16), jax.ShapeDtypeStruct(s0.shape, F32)),
        grid=(n_seq // sps, n_qk),
        in_specs=in_specs,
        out_specs=(pl.BlockSpec((tr, vw), lambda g, h: (rb0 + g, h)), s_spec),
        input_output_aliases=aliases,
        compiler_params=_params(("arbitrary", "arbitrary"), 32),
        name="gated_delta_sample",
    )(*args)


def _router_kernel(x_ref, nw_ref, wr_ref, br_ref, h_ref, sel_ref, gate_ref, cnt_ref, cnt_sc):
    tm = x_ref.shape[0]

    @pl.when(pl.program_id(0) == 0)
    def _():
        cnt_sc[...] = jnp.zeros_like(cnt_sc)

    h = _rms(x_ref[...], nw_ref[...])
    h_ref[...] = h
    h_hi = h.astype(BF16)
    h_lo = (h - h_hi.astype(F32)).astype(BF16)
    h3 = jnp.concatenate([h_hi, h_hi, h_lo], axis=1)
    logits = jnp.dot(h3, wr_ref[...], preferred_element_type=F32) + br_ref[...]
    lane = lax.broadcasted_iota(I32, (tm, LANES), 1)
    lane_f = lane.astype(F32)
    big = float(LANES)

    is_g = lane < N_GROUPS
    gl = jnp.where(is_g, logits, -jnp.inf)
    ge = jnp.exp(gl - jnp.max(gl, axis=-1, keepdims=True))
    gp = ge / jnp.sum(ge, axis=-1, keepdims=True)
    p_group = jnp.max(gp, axis=-1, keepdims=True)
    g_sel = jnp.min(jnp.where(is_g & (gp == p_group), lane_f, big), axis=-1, keepdims=True)

    e_lane = lane - N_GROUPS
    in_grp = (e_lane >= 0) & (e_lane < N_EXPERTS) & ((e_lane >> 3).astype(F32) == g_sel)
    el = jnp.where(in_grp, logits, -jnp.inf)
    ee = jnp.exp(el - jnp.max(el, axis=-1, keepdims=True))
    ep = ee / jnp.sum(ee, axis=-1, keepdims=True)
    p1 = jnp.max(ep, axis=-1, keepdims=True)
    i1 = jnp.min(jnp.where(in_grp & (ep == p1), lane_f, big), axis=-1, keepdims=True)
    rest = in_grp & (lane_f != i1)
    p2 = jnp.max(jnp.where(rest, ep, -1.0), axis=-1, keepdims=True)
    i2 = jnp.min(jnp.where(rest & (ep == p2), lane_f, big), axis=-1, keepdims=True)
    denom = p1 + p2
    gate1 = p_group * p1 / denom
    gate2 = p_group * p2 / denom
    e1 = i1 - float(N_GROUPS)
    e2 = i2 - float(N_GROUPS)

    onehot = (lane_f == e1) | (lane_f == e2)
    oh = jnp.where(onehot, 1.0, 0.0)
    rr = lax.broadcasted_iota(I32, (tm, tm), 0)
    cc = lax.broadcasted_iota(I32, (tm, tm), 1)
    tri = jnp.where(cc < rr, 1.0, 0.0).astype(BF16)
    before = jnp.dot(tri, oh.astype(BF16), preferred_element_type=F32) + cnt_sc[...]
    r1 = jnp.sum(jnp.where(lane_f == e1, before, 0.0), axis=-1, keepdims=True)
    r2 = jnp.sum(jnp.where(lane_f == e2, before, 0.0), axis=-1, keepdims=True)
    cnt_sc[...] = cnt_sc[...] + jnp.sum(oh, axis=0, keepdims=True)
    cnt_ref[...] = cnt_sc[...]

    sel = jnp.where(lane == 0, e1, jnp.where(lane == 1, e2, jnp.where(lane == 2, r1, jnp.where(lane == 3, r2, 0.0))))
    sel_ref[...] = sel.astype(I32)
    gate_ref[...] = jnp.where(lane == 0, gate1, jnp.where(lane == 1, gate2, 0.0))


def moe_router(x, norm_w, w_route, b_route):
    n, d = x.shape
    tm = _pick(n, 256)
    return pl.pallas_call(
        _router_kernel,
        out_shape=(jax.ShapeDtypeStruct((n, d), F32),
                   jax.ShapeDtypeStruct((n, LANES), I32),
                   jax.ShapeDtypeStruct((n, LANES), F32),
                   jax.ShapeDtypeStruct((1, LANES), F32)),
        grid=(n // tm,),
        in_specs=[pl.BlockSpec((tm, d), lambda i: (i, 0)),
                  pl.BlockSpec((1, d), lambda i: (0, 0)),
                  pl.BlockSpec((3 * d, LANES), lambda i: (0, 0)),
                  pl.BlockSpec((1, LANES), lambda i: (0, 0))],
        out_specs=(pl.BlockSpec((tm, d), lambda i: (i, 0)),
                   pl.BlockSpec((tm, LANES), lambda i: (i, 0)),
                   pl.BlockSpec((tm, LANES), lambda i: (i, 0)),
                   pl.BlockSpec((1, LANES), lambda i: (0, 0))),
        scratch_shapes=[pltpu.VMEM((1, LANES), F32)],
        compiler_params=_params(("arbitrary",), 32),
        name="moe_router",
    )(x, norm_w.reshape(1, d), w_route, b_route)


ISSUE_UNROLL = 8


def _slot(slot_ref, r, k):
    return slot_ref[0, 0, TOP_K * r + k]


def _row_copy_out(h_ref, xs_ref, sem, r, d):
    return pltpu.make_async_copy(h_ref.at[pl.ds(r, 1), :], xs_ref.at[pl.ds(d, 1), :], sem)


def _dispatch_kernel(slot_ref, h_ref, xs_in_ref, xs_ref, sem):
    del xs_in_ref
    tm = h_ref.shape[0]

    def issue(r, carry):
        for k in range(TOP_K):
            _row_copy_out(h_ref, xs_ref, sem, r, _slot(slot_ref, r, k)).start()
        return carry

    lax.fori_loop(0, tm, issue, 0, unroll=ISSUE_UNROLL)
    for k in range(TOP_K):
        pltpu.make_async_copy(h_ref, xs_ref.at[pl.ds(0, tm), :], sem).wait()


def moe_dispatch(h, slot3, n_slots):
    n, d = h.shape
    tm = slot3.shape[2] // TOP_K
    xs0 = jnp.zeros((n_slots, d), F32)
    return pl.pallas_call(
        _dispatch_kernel,
        out_shape=jax.ShapeDtypeStruct((n_slots, d), F32),
        grid=(n // tm,),
        in_specs=[pl.BlockSpec((1, 1, TOP_K * tm), lambda i: (i, 0, 0), memory_space=pltpu.SMEM),
                  pl.BlockSpec((tm, d), lambda i: (i, 0)),
                  pl.BlockSpec(memory_space=pl.ANY)],
        out_specs=pl.BlockSpec(memory_space=pl.ANY),
        scratch_shapes=[pltpu.SemaphoreType.DMA(())],
        input_output_aliases={2: 0},
        compiler_params=_params(("arbitrary",), 32),
        name="moe_dispatch",
    )(slot3, h, xs0)


def _ffn_kernel(be_ref, first_ref, nused_ref, x_ref, wg_ref, wu_ref, wd_ref, o_ref, wgb, wub, wdb):
    b = pl.program_id(0)
    valid = b < nused_ref[0]

    @pl.when(valid & (first_ref[b] == 1))
    def _():
        wgb[...] = wg_ref[...].astype(BF16)
        wub[...] = wu_ref[...].astype(BF16)
        wdb[...] = wd_ref[...].astype(BF16)

    @pl.when(valid)
    def _():
        x = x_ref[...].astype(BF16)
        gate = jnp.dot(x, wgb[...], preferred_element_type=F32)
        up = jnp.dot(x, wub[...], preferred_element_type=F32)
        hid = (_silu(gate) * up).astype(BF16)
        o_ref[...] = jnp.dot(hid, wdb[...], preferred_element_type=F32)

    @pl.when(jnp.logical_not(valid))
    def _():
        o_ref[...] = jnp.zeros_like(o_ref)


def moe_experts(xs, block_expert, block_first, n_used, w_gate, w_up, w_down, layer):
    n_slots, d = xs.shape
    de = w_gate.shape[-1]
    n_blocks = n_slots // MOE_ROWS
    xmap = lambda b, be, fi, nu: (jnp.minimum(b, jnp.maximum(nu[0] - 1, 0)), 0)
    wmap = lambda b, be, fi, nu: (layer, be[b], 0, 0)
    grid_spec = pltpu.PrefetchScalarGridSpec(
        num_scalar_prefetch=3,
        grid=(n_blocks,),
        in_specs=[pl.BlockSpec((MOE_ROWS, d), xmap),
                  pl.BlockSpec((None, None, d, de), wmap),
                  pl.BlockSpec((None, None, d, de), wmap),
                  pl.BlockSpec((None, None, de, d), wmap)],
        out_specs=pl.BlockSpec((MOE_ROWS, d), lambda b, be, fi, nu: (b, 0)),
        scratch_shapes=[pltpu.VMEM((d, de), BF16), pltpu.VMEM((d, de), BF16), pltpu.VMEM((de, d), BF16)],
    )
    return pl.pallas_call(
        _ffn_kernel,
        out_shape=jax.ShapeDtypeStruct((n_slots, d), F32),
        grid_spec=grid_spec,
        compiler_params=_params(("arbitrary",), 48),
        name="moe_experts",
    )(block_expert, block_first, n_used, xs, w_gate, w_up, w_down)


def _row_copy_in(yb_ref, buf_ref, sem, k, r, d):
    return pltpu.make_async_copy(yb_ref.at[pl.ds(d, 1), :], buf_ref.at[k, pl.ds(r, 1), :], sem)


def _combine_kernel(slot_ref, x_ref, gate_ref, nw_ref, yb_ref, o1_ref, o2_ref, buf_ref, sem, *, split_step):
    tm = x_ref.shape[0]

    def issue(r, carry):
        for k in range(TOP_K):
            _row_copy_in(yb_ref, buf_ref, sem, k, r, _slot(slot_ref, r, k)).start()
        return carry

    lax.fori_loop(0, tm, issue, 0, unroll=ISSUE_UNROLL)
    for k in range(TOP_K):
        pltpu.make_async_copy(yb_ref.at[pl.ds(0, tm), :], buf_ref.at[k], sem).wait()

    g = gate_ref[...]
    y = buf_ref[0] * g[:, 0:1] + buf_ref[1] * g[:, 1:2]
    xn = x_ref[...] + y
    hn = _rms(xn, nw_ref[...])
    if split_step is None:
        o1_ref[...] = xn
        o2_ref[...] = hn.astype(o2_ref.dtype)
    else:
        @pl.when(pl.program_id(0) < split_step)
        def _():
            o1_ref[...] = hn

        @pl.when(pl.program_id(0) >= split_step)
        def _():
            o2_ref[...] = hn


def moe_combine(x, yb, slot3, gates, next_norm_w, split_rows=None):
    n, d = x.shape
    tm = slot3.shape[2] // TOP_K
    if split_rows is None:
        split_step = None
        out_shape = (jax.ShapeDtypeStruct((n, d), F32), jax.ShapeDtypeStruct((n, d), BF16))
        out_specs = (pl.BlockSpec((tm, d), lambda i: (i, 0)), pl.BlockSpec((tm, d), lambda i: (i, 0)))
    else:
        assert split_rows % tm == 0 and 0 < split_rows < n
        split_step = split_rows // tm
        out_shape = (jax.ShapeDtypeStruct((split_rows, d), F32), jax.ShapeDtypeStruct((n - split_rows, d), F32))
        out_specs = (pl.BlockSpec((tm, d), lambda i: (jnp.minimum(i, split_step - 1), 0)),
                     pl.BlockSpec((tm, d), lambda i: (jnp.maximum(i - split_step, 0), 0)))
    return pl.pallas_call(
        functools.partial(_combine_kernel, split_step=split_step),
        out_shape=out_shape,
        grid=(n // tm,),
        in_specs=[pl.BlockSpec((1, 1, TOP_K * tm), lambda i: (i, 0, 0), memory_space=pltpu.SMEM),
                  pl.BlockSpec((tm, d), lambda i: (i, 0)),
                  pl.BlockSpec((tm, LANES), lambda i: (i, 0)),
                  pl.BlockSpec((1, d), lambda i: (0, 0)),
                  pl.BlockSpec(memory_space=pl.ANY)],
        out_specs=out_specs,
        scratch_shapes=[pltpu.VMEM((TOP_K, tm, d), F32), pltpu.SemaphoreType.DMA(())],
        compiler_params=_params(("arbitrary",), 40),
        name="moe_combine",
    )(slot3, x, gates, next_norm_w.reshape(1, d), yb)


def hier_moe(x, ffn_norm_w, w_route, b_route, w_gate, w_up, w_down, layer, next_norm_w, split_rows=None):
    n, _ = x.shape
    h, sel, gates, cnt = moe_router(x, ffn_norm_w, w_route, b_route)
    counts = cnt[0, :N_EXPERTS].astype(I32)
    padded = (counts + MOE_ROWS - 1) // MOE_ROWS * MOE_ROWS
    pend = jnp.cumsum(padded)
    pstart = (pend - padded).astype(I32)
    tm = _pick(n, 256)
    expert, rank = sel[:, :TOP_K], sel[:, TOP_K:2 * TOP_K]
    is_e = expert[:, :, None] == jnp.arange(N_EXPERTS, dtype=I32)
    slot3 = (rank + jnp.sum(jnp.where(is_e, pstart, 0), axis=-1)).reshape(n // tm, 1, TOP_K * tm)
    n_blocks = -(-(n * TOP_K) // MOE_ROWS) + N_EXPERTS
    n_used = pend[-1] // MOE_ROWS
    blk = jnp.arange(n_blocks, dtype=I32)
    last = jnp.maximum(n_used - 1, 0)
    bstart = jnp.minimum(blk, last) * MOE_ROWS
    block_expert = jnp.minimum(jnp.sum(pend[None, :] <= bstart[:, None], axis=1), N_EXPERTS - 1).astype(I32)
    block_first = (bstart == pstart[block_expert]).astype(I32)
    xs = moe_dispatch(h, slot3, n_blocks * MOE_ROWS)
    yb = moe_experts(xs, block_expert, block_first, n_used.reshape(1).astype(I32), w_gate, w_up, w_down, layer)
    return moe_combine(x, yb, slot3, gates, next_norm_w, split_rows)


def kernel(x_prompt, x_sample, state_delta_S, state_delta_conv, norm_mixer_w, norm_ffn_w, norm_final_w,
           a_w_in, a_ln_g, a_ln_b, a_w_s, a_b_s, a_w_out,
           b_w_in, b_conv_w, b_a_log, b_dt_bias, b_norm_w, b_w_out,
           moe_w_group, moe_b_group, moe_w_router, moe_b_router, moe_w_gate, moe_w_up, moe_w_down):
    nb, seq, d = x_prompt.shape
    db, dseq, _ = x_sample.shape
    depth = norm_mixer_w.shape[0]
    n_p, n_s = nb * seq, db * dseq
    n = n_p + n_s
    a_width = a_ln_g.shape[1]
    n_groups_a = a_width // A_GROUP_DIM
    n_heads = b_a_log.shape[1]
    value_dim = n_heads * DV
    conv_dim = b_conv_w.shape[2]
    key_dim = (conv_dim - value_dim) // 2
    assert seq % CHUNK_A == 0 and n_s % CHUNK_A == 0 and CHUNK_A % dseq == 0 and dseq == SUBLANES
    assert seq % CHUNK_B == 0 and dseq <= CHUNK_B and dseq >= CONV_W - 1


    causal = jnp.tril(jnp.ones((CHUNK_A, CHUNK_A), bool))
    ws_p = jnp.where(causal, a_w_s, 0.0)
    per = CHUNK_A // dseq
    ws_s = jnp.where(causal[:dseq, :dseq], a_w_s[:, :, :dseq, :dseq], 0.0)
    ws_s = jnp.einsum("ij,lgts->lgitjs", jnp.eye(per, dtype=F32), ws_s).reshape(a_w_s.shape)
    ws2 = jnp.stack([ws_p, ws_s], axis=1).astype(BF16)
    bias_p = jnp.repeat(jnp.swapaxes(a_b_s, 1, 2), A_GROUP_DIM, axis=2)
    bias_s = jnp.tile(bias_p[:, :dseq], (1, per, 1))
    bias2 = jnp.stack([bias_p, bias_s], axis=1)

    w_route = jnp.concatenate([moe_w_group, moe_w_router,
                               jnp.zeros((depth, d, LANES - N_GROUPS - N_EXPERTS), F32)], axis=2)
    w_route_hi = w_route.astype(BF16)
    w_route_lo = (w_route - w_route_hi.astype(F32)).astype(BF16)
    w_route = jnp.concatenate([w_route_hi, w_route_lo, w_route_hi], axis=1)
    b_route = jnp.concatenate([moe_b_group, moe_b_router,
                               jnp.zeros((depth, LANES - N_GROUPS - N_EXPERTS), F32)], axis=1)
    b_w_in_t = jnp.swapaxes(b_w_in, 1, 2)
    zeros_h = jnp.zeros_like(b_a_log)
    a_log2 = jnp.concatenate([zeros_h, b_a_log], axis=1)
    dt2 = jnp.concatenate([zeros_h, b_dt_bias], axis=1)

    chunk_v, s_prompt, conv_prompt, conv_sample = [], [], [], []
    s_sample = None
    x, h = concat_rmsnorm(x_prompt.reshape(n_p, d), x_sample.reshape(n_s, d), norm_mixer_w[0])
    for i in range(depth):
        j = i // 2
        if i % 2 == 0:
            u = matmul(h, a_w_in, j, 0, a_width, BF16, act="gelu")
            vpre = matmul(h, a_w_in, j, a_width, a_width, F32, act="gelu")
            gated, v_s = gmlp_gate(u, vpre, a_ln_g[j], a_ln_b[j], ws2[j], bias2[j], n_p // CHUNK_A)
            chunk_v.append(v_s.reshape(db, dseq, a_width))
            x = matmul(gated, a_w_out, j, 0, d, F32, res=x)
        else:
            qkv = matmul(h, b_w_in_t, j, 0, conv_dim, F32, w_t=True)
            z = matmul(h, b_w_in_t, j, conv_dim, value_dim, BF16, w_t=True)
            gb = matmul(h, b_w_in_t, j, conv_dim + value_dim, 2 * n_heads, F32, act="gdn_gates",
                        extra=(a_log2[j:j + 1], dt2[j:j + 1]), w_t=True)
            gt = gb[:, n_heads:].T
            keep = CONV_W - 1
            conv_prompt.append(jnp.stack([qkv[(b + 1) * seq - keep:(b + 1) * seq] for b in range(nb)]))
            conv_sample.append(qkv[n_p:].reshape(db, dseq, conv_dim)[:, dseq - keep:])
            pad8 = ((0, 0), (SUBLANES - keep, 0), (0, 0))
            st8_p = jnp.zeros((nb * SUBLANES, conv_dim), F32)
            st8_s = jnp.pad(state_delta_conv[j], pad8).reshape(db * SUBLANES, conv_dim)
            qc_p = gdn_conv_prompt(qkv, st8_p, b_conv_w, j, nb, seq, key_dim)
            qc_s = gdn_conv_sample(qkv, n_p, st8_s, b_conv_w, j, n_s, dseq, key_dim)
            o_p, s_p = gated_delta_prompt(qc_p, z, gb, gt, b_norm_w[j], nb, seq, key_dim, n_heads)
            gated, s_sample = gated_delta_sample(qc_s, z, gb, gt, b_norm_w[j], state_delta_S, j, n_p, db, dseq,
                                                 key_dim, o_p, s_sample)
            s_prompt.append(s_p)
            x = matmul(gated, b_w_out, j, 0, d, F32, res=x)
        moe_w = (norm_ffn_w[i], w_route[i], b_route[i:i + 1], moe_w_gate, moe_w_up, moe_w_down, i)
        if i < depth - 1:
            x, h = hier_moe(x, *moe_w, norm_mixer_w[i + 1])
        else:
            y_p, y_s = hier_moe(x, *moe_w, norm_final_w, split_rows=n_p)

    y_prompt = y_p.reshape(nb, seq, d)
    y_sample = y_s.reshape(db, dseq, d)
    return (y_prompt, y_sample, jnp.stack(chunk_v), jnp.stack(s_prompt), jnp.stack(conv_prompt),
            s_sample, jnp.stack(conv_sample))
```

```python
import functools
import math

import jax
import jax.numpy as jnp
from jax import lax
from jax.experimental import pallas as pl
from jax.experimental.pallas import tpu as pltpu

F32 = jnp.float32
BF16 = jnp.bfloat16
I32 = jnp.int32

EPS = 1e-6
LANES = 128
SUBLANES = 8
CHUNK_A = 128
A_GROUP_DIM = 128
DK = 128
DV = 128
CONV_W = 4
CHUNK_B = 64
N_GROUPS = 8
EXPERTS_PER_GROUP = 8
N_EXPERTS = N_GROUPS * EXPERTS_PER_GROUP
TOP_K = 2
MOE_ROWS = 256
MIB = 2 ** 20


def _params(semantics, vmem_mib):
    return pltpu.CompilerParams(dimension_semantics=semantics, vmem_limit_bytes=vmem_mib * MIB)


def _pick(n, pref):
    t = min(n, pref)
    while n % t:
        t -= SUBLANES
    return t


def _rms(x, w):
    return x * lax.rsqrt(jnp.mean(x * x, axis=-1, keepdims=True) + EPS) * w


def _silu(x):
    return x * jax.nn.sigmoid(x)


def _concat_rmsnorm_kernel(xa_ref, xb_ref, w_ref, x_ref, h_ref, *, steps_a):
    @pl.when(pl.program_id(0) < steps_a)
    def _():
        x_ref[...] = xa_ref[...]

    @pl.when(pl.program_id(0) >= steps_a)
    def _():
        x_ref[...] = xb_ref[...]

    h_ref[...] = _rms(x_ref[...], w_ref[...]).astype(h_ref.dtype)


def concat_rmsnorm(xa, xb, w):
    (na, d), (nb_, _) = xa.shape, xb.shape
    tm = _pick(math.gcd(na, nb_), 512)
    steps_a = na // tm
    return pl.pallas_call(
        functools.partial(_concat_rmsnorm_kernel, steps_a=steps_a),
        out_shape=(jax.ShapeDtypeStruct((na + nb_, d), F32), jax.ShapeDtypeStruct((na + nb_, d), BF16)),
        grid=((na + nb_) // tm,),
        in_specs=[pl.BlockSpec((tm, d), lambda i: (jnp.minimum(i, steps_a - 1), 0)),
                  pl.BlockSpec((tm, d), lambda i: (jnp.maximum(i - steps_a, 0), 0)),
                  pl.BlockSpec((1, d), lambda i: (0, 0))],
        out_specs=(pl.BlockSpec((tm, d), lambda i: (i, 0)), pl.BlockSpec((tm, d), lambda i: (i, 0))),
        compiler_params=_params(("arbitrary",), 32),
        name="concat_rmsnorm",
    )(xa, xb, w.reshape(1, d))


def _mm_kernel(*refs, act, has_res, n_extra, w_t):
    a_ref, w_ref = refs[:2]
    extra = refs[2:2 + n_extra]
    rest = refs[2 + n_extra:]
    if has_res:
        r_ref, o_ref, wb_ref = rest
    else:
        o_ref, wb_ref = rest

    @pl.when(pl.program_id(1) == 0)
    def _():
        wb_ref[...] = w_ref[...].astype(BF16)

    if w_t:
        acc = _dot_nt(a_ref[...], wb_ref[...])
    else:
        acc = jnp.dot(a_ref[...], wb_ref[...], preferred_element_type=F32)
    if act == "gelu":
        acc = jax.nn.gelu(acc)
    elif act == "gdn_gates":
        a_log_ref, dt_ref = extra
        h = acc.shape[1] // 2
        lane = lax.broadcasted_iota(I32, acc.shape, 1)
        g = -jnp.exp(a_log_ref[...]) * jax.nn.softplus(acc + dt_ref[...])
        acc = jnp.where(lane < h, jax.nn.sigmoid(acc), g)
    if has_res:
        acc = acc + r_ref[...]
    o_ref[...] = acc.astype(o_ref.dtype)


def matmul(a, w, layer, col0, ncols, out_dtype, *, act=None, res=None, extra=(), tm=1024, tn=512, w_t=False):
    m, k = a.shape
    tm = _pick(m, tm)
    if k <= 2048 and ncols % (2 * tn) == 0 and col0 % (2 * tn) == 0:
        tn = 2 * tn
    tn = min(tn, ncols)
    assert ncols % tn == 0 and col0 % tn == 0
    c0 = col0 // tn
    if w_t:
        w_spec = pl.BlockSpec((None, tn, k), lambda j, i: (layer, c0 + j, 0))
    else:
        w_spec = pl.BlockSpec((None, k, tn), lambda j, i: (layer, 0, c0 + j))
    in_specs = [pl.BlockSpec((tm, k), lambda j, i: (i, 0)), w_spec]
    args = [a, w]
    for e in extra:
        in_specs.append(pl.BlockSpec((1, tn), lambda j, i: (0, j)))
        args.append(e)
    if res is not None:
        in_specs.append(pl.BlockSpec((tm, tn), lambda j, i: (i, j)))
        args.append(res)
    out_bytes = jnp.dtype(out_dtype).itemsize
    vmem = (2 * tm * k * 2 + 2 * k * tn * 4 + k * tn * 2 + 2 * tm * tn * out_bytes
            + (2 * tm * tn * 4 if res is not None else 0) + 3 * tm * tn * 4)
    return pl.pallas_call(
        functools.partial(_mm_kernel, act=act, has_res=res is not None, n_extra=len(extra), w_t=w_t),
        out_shape=jax.ShapeDtypeStruct((m, ncols), out_dtype),
        grid=(ncols // tn, m // tm),
        in_specs=in_specs,
        out_specs=pl.BlockSpec((tm, tn), lambda j, i: (i, j)),
        scratch_shapes=[pltpu.VMEM((tn, k) if w_t else (k, tn), BF16)],
        compiler_params=_params(("arbitrary", "arbitrary"), min(56, vmem // MIB + 8)),
        name="matmul_" + (act or "plain") + ("_res" if res is not None else ""),
    )(*args)


def _gmlp_gate_kernel(u_ref, v_ref, g_ref, b_ref, ws_ref, bias_ref, o_ref, vo_ref, *, n_groups, first_sample):
    v = v_ref[...]
    xc = v - jnp.mean(v, axis=-1, keepdims=True)
    var = jnp.mean(xc * xc, axis=-1, keepdims=True)
    vn = xc * lax.rsqrt(var + EPS) * g_ref[...] + b_ref[...]

    @pl.when(pl.program_id(0) >= first_sample)
    def _():
        vo_ref[...] = vn

    vb = vn.astype(BF16)
    for g in range(n_groups):
        sl = slice(g * A_GROUP_DIM, (g + 1) * A_GROUP_DIM)
        s = jnp.dot(ws_ref[g], vb[:, sl], preferred_element_type=F32) + bias_ref[:, sl]
        o_ref[:, sl] = (u_ref[:, sl].astype(F32) * s).astype(BF16)


def gmlp_gate(u, vpre, ln_g, ln_b, ws2, bias2, n_prompt_chunks):
    n, aw = u.shape
    n_chunks = n // CHUNK_A
    n_groups = aw // A_GROUP_DIM
    n_sample_rows = n - n_prompt_chunks * CHUNK_A
    which = lambda c: jnp.where(c >= n_prompt_chunks, 1, 0)
    return pl.pallas_call(
        functools.partial(_gmlp_gate_kernel, n_groups=n_groups, first_sample=n_prompt_chunks),
        out_shape=(jax.ShapeDtypeStruct((n, aw), BF16),
                   jax.ShapeDtypeStruct((n_sample_rows, aw), F32)),
        grid=(n_chunks,),
        in_specs=[pl.BlockSpec((CHUNK_A, aw), lambda c: (c, 0)),
                  pl.BlockSpec((CHUNK_A, aw), lambda c: (c, 0)),
                  pl.BlockSpec((1, aw), lambda c: (0, 0)),
                  pl.BlockSpec((1, aw), lambda c: (0, 0)),
                  pl.BlockSpec((None, n_groups, CHUNK_A, CHUNK_A), lambda c: (which(c), 0, 0, 0)),
                  pl.BlockSpec((None, CHUNK_A, aw), lambda c: (which(c), 0, 0))],
        out_specs=(pl.BlockSpec((CHUNK_A, aw), lambda c: (c, 0)),
                   pl.BlockSpec((CHUNK_A, aw), lambda c: (jnp.maximum(c - n_prompt_chunks, 0), 0))),
        compiler_params=_params(("arbitrary",), 40),
        name="gmlp_gate",
    )(u, vpre, ln_g.reshape(1, aw), ln_b.reshape(1, aw), ws2, bias2)


def _conv_epilogue(y, o_ref, rows, n_q_blocks, n_qk_blocks):
    c = pl.program_id(1)
    y = _silu(y)
    tc = y.shape[1]

    @pl.when(c < n_qk_blocks)
    def _():
        scale = jnp.where(c < n_q_blocks, DK ** -0.5, 1.0).astype(F32)
        for j in range(tc // DK):
            blk = y[:, j * DK:(j + 1) * DK]
            nrm = blk * lax.rsqrt(jnp.sum(blk * blk, axis=-1, keepdims=True) + EPS)
            o_ref[rows, j * DK:(j + 1) * DK] = (nrm * scale).astype(o_ref.dtype)

    @pl.when(c >= n_qk_blocks)
    def _():
        o_ref[rows, :] = y.astype(o_ref.dtype)


def _conv_prompt_kernel(x_ref, st_ref, w_ref, o_ref, *, sub, n_q_blocks, n_qk_blocks):
    t_len, tc = x_ref.shape
    w = w_ref[...]
    r8 = lax.broadcasted_iota(I32, (SUBLANES, tc), 0)

    def body(i, carry):
        r0 = pl.multiple_of(i * sub, sub)
        cur = x_ref[pl.ds(r0, sub), :]
        p0 = pl.multiple_of(jnp.maximum(r0 - SUBLANES, 0), SUBLANES)
        prev8 = jnp.where(i == 0, st_ref[...], x_ref[pl.ds(p0, SUBLANES), :])
        acc = cur * w[CONV_W - 1:CONV_W, :]
        for k in range(1, CONV_W):
            xs = pltpu.roll(cur, k, 0)
            top = jnp.where(r8 < k, pltpu.roll(prev8, k, 0), xs[:SUBLANES])
            if sub > SUBLANES:
                xs = jnp.concatenate([top, xs[SUBLANES:]], axis=0)
            else:
                xs = top
            acc = acc + xs * w[CONV_W - 1 - k:CONV_W - k, :]
        _conv_epilogue(acc, o_ref, pl.ds(r0, sub), n_q_blocks, n_qk_blocks)
        return carry

    lax.fori_loop(0, t_len // sub, body, 0)


def _conv_sample_kernel(x_ref, st_ref, w_ref, o_ref, *, seq_len, n_q_blocks, n_qk_blocks):
    rows, tc = x_ref.shape
    assert seq_len == SUBLANES
    w = w_ref[...]
    x = x_ref[...]
    st = st_ref[...]
    t = lax.broadcasted_iota(I32, (rows, tc), 0) & (seq_len - 1)
    acc = x * w[CONV_W - 1:CONV_W, :]
    for k in range(1, CONV_W):
        xs = pltpu.roll(x, k, 0)
        ss = pltpu.roll(st, rows - (seq_len - k), 0)
        acc = acc + jnp.where(t >= k, xs, ss) * w[CONV_W - 1 - k:CONV_W - k, :]
    _conv_epilogue(acc, o_ref, slice(None), n_q_blocks, n_qk_blocks)


def gdn_conv_prompt(qkv, st8, conv_w, layer, n_seq, seq_len, key_dim):
    c_dim = qkv.shape[1]
    tc = 256
    sub = _pick(seq_len, 128)
    nqb, nqkb = key_dim // tc, 2 * key_dim // tc
    return pl.pallas_call(
        functools.partial(_conv_prompt_kernel, sub=sub, n_q_blocks=nqb, n_qk_blocks=nqkb),
        out_shape=jax.ShapeDtypeStruct((n_seq * seq_len, c_dim), BF16),
        grid=(n_seq, c_dim // tc),
        in_specs=[pl.BlockSpec((seq_len, tc), lambda b, c: (b, c)),
                  pl.BlockSpec((SUBLANES, tc), lambda b, c: (b, c)),
                  pl.BlockSpec((None, CONV_W, tc), lambda b, c: (layer, 0, c))],
        out_specs=pl.BlockSpec((seq_len, tc), lambda b, c: (b, c)),
        compiler_params=_params(("arbitrary", "arbitrary"), 32),
        name="gdn_conv_prompt",
    )(qkv, st8, conv_w)


def gdn_conv_sample(qkv, row0, st8, conv_w, layer, n_rows, seq_len, key_dim):
    c_dim = qkv.shape[1]
    tc = 512
    tr = _pick(n_rows, 256)
    assert row0 % tr == 0
    rb0 = row0 // tr
    nqb, nqkb = key_dim // tc, 2 * key_dim // tc
    return pl.pallas_call(
        functools.partial(_conv_sample_kernel, seq_len=seq_len, n_q_blocks=nqb, n_qk_blocks=nqkb),
        out_shape=jax.ShapeDtypeStruct((n_rows, c_dim), BF16),
        grid=(n_rows // tr, c_dim // tc),
        in_specs=[pl.BlockSpec((tr, tc), lambda r, c: (rb0 + r, c)),
                  pl.BlockSpec((tr, tc), lambda r, c: (r, c)),
                  pl.BlockSpec((None, CONV_W, tc), lambda r, c: (layer, 0, c))],
        out_specs=pl.BlockSpec((tr, tc), lambda r, c: (r, c)),
        compiler_params=_params(("arbitrary", "arbitrary"), 32),
        name="gdn_conv_sample",
    )(qkv, st8, conv_w)


def _dot_nt(a, b):
    return lax.dot_general(a, b, (((1,), (1,)), ((), ())), preferred_element_type=F32)


def _dot_tn(a, b):
    return lax.dot_general(a, b, (((0,), (0,)), ((), ())), preferred_element_type=F32)


def _bdot(a, b):
    return jnp.dot(a.astype(BF16), b.astype(BF16), preferred_element_type=F32)


def _chunk_masks(c_len, seq_len=None):
    ri = lax.broadcasted_iota(I32, (c_len, 2 * c_len), 0)
    ci = lax.broadcasted_iota(I32, (c_len, 2 * c_len), 1)
    left = ci < c_len
    if seq_len is not None:
        shift = seq_len.bit_length() - 1
        left = left & ((ri >> shift) == (ci >> shift))
    sq = lambda m: m[:, :c_len]
    return dict(incl=sq(left & (ci <= ri)), incl_w=left & (ci <= ri), strict_w=left & (ci < ri),
                incl_t_w=left & (ri <= ci), eye_w=jnp.where(ci == ri + c_len, 1.0, 0.0),
                left_w=ci < c_len, same=sq(left) if seq_len is not None else None)


def _chunk_setup(kk_w, qkm, kf, qf, vf, gb, gt, h, n_heads, masks):
    rows = qkm.shape[0]
    lane_gb = lax.broadcasted_iota(I32, (rows, 2 * n_heads), 1)
    sub_gt = lax.broadcasted_iota(I32, (n_heads, rows), 0)
    beta = jnp.sum(jnp.where(lane_gb == h, gb, 0.0), axis=-1, keepdims=True)
    g_col = jnp.sum(jnp.where(lane_gb == h + n_heads, gb, 0.0), axis=-1, keepdims=True)
    g_row = jnp.sum(jnp.where(sub_gt == h, gt, 0.0), axis=0, keepdims=True)
    gcum = jnp.sum(jnp.where(masks["incl"], g_row, 0.0), axis=-1, keepdims=True)
    gcum_row = jnp.sum(jnp.where(masks["incl_t_w"], g_col, 0.0), axis=0, keepdims=True)
    decay_w = jnp.exp(jnp.where(masks["incl_w"], gcum - gcum_row, -jnp.inf))
    egc = jnp.exp(gcum)
    out = dict(w0=masks["eye_w"] - jnp.where(masks["strict_w"], kk_w * decay_w * beta, 0.0),
               rhs=jnp.concatenate([vf * beta, kf * (beta * egc)], axis=1),
               qk=qkm * decay_w[:, :rows], q_g=qf * egc, gcum=gcum, kf=kf)
    if masks["same"] is not None:
        out["gsum"] = jnp.sum(jnp.where(masks["same"], g_row, 0.0), axis=-1, keepdims=True)
    return out


def _product_step(w, left_w):
    wb = w.astype(BF16)
    prod = jnp.dot(wb, jnp.concatenate([wb, jnp.zeros_like(wb)], axis=0), preferred_element_type=F32)
    return jnp.where(left_w, prod, w + prod)


def _apply_inverse(w, rhs):
    rb = rhs.astype(BF16)
    return jnp.dot(w.astype(BF16), jnp.concatenate([jnp.zeros_like(rb), rb], axis=0), preferred_element_type=F32)


def _delta_kernel(q_ref, k_ref, v_ref, z_ref, gb_ref, gt_ref, nw_ref, o_init_ref, o_ref, s_ref, *,
                  n_chunks, n_heads, heads_per_qk, qk_per_step):
    c_len = CHUNK_B
    hpb = pl.program_id(1)
    hps = heads_per_qk
    n_local = qk_per_step * hps

    @pl.when(pl.program_id(2) == 0)
    def _():
        s_ref[...] = jnp.zeros_like(s_ref)

    masks = _chunk_masks(c_len)
    nw = nw_ref[...]

    def prep_stages(chunks):
        st = {}

        def setup():
            items = {}
            for c in chunks:
                rs = slice(c * c_len, (c + 1) * c_len)
                gb = gb_ref[rs, :]
                gt = gt_ref[:, rs]
                for qh in range(qk_per_step):
                    kc = k_ref[rs, qh * DK:(qh + 1) * DK]
                    qc = q_ref[rs, qh * DK:(qh + 1) * DK]
                    kk_w = _dot_nt(kc, jnp.concatenate([kc, jnp.zeros_like(kc)], axis=0))
                    qkm = _dot_nt(qc, kc)
                    kf = kc.astype(F32)
                    qf = qc.astype(F32)
                    for hh in range(hps):
                        hl = qh * hps + hh
                        vf = v_ref[rs, hl * DV:(hl + 1) * DV].astype(F32)
                        items[(c, hl)] = _chunk_setup(kk_w, qkm, kf, qf, vf, gb, gt, hpb * n_local + hl,
                                                      n_heads, masks)
            st["prep"] = items
            st["w"] = {key: p["w0"] for key, p in items.items()}

        def step():
            st["w"] = {key: _product_step(w, masks["left_w"]) for key, w in st["w"].items()}

        def solve():
            st["sols"] = {key: _apply_inverse(w, st["prep"][key]["rhs"]) for key, w in st["w"].items()}

        return st, [setup] + [step] * (c_len.bit_length() - 1) + [solve]

    state = [s_ref[0, hl] for hl in range(n_local)]

    def chain_stages(chunks, st):
        stages = []
        for c in chunks:
            tmp = {}

            def apply_state(c=c, tmp=tmp):
                tmp["ws"] = [_bdot(jnp.concatenate([st["sols"][(c, hl)][:, DV:], st["prep"][(c, hl)]["q_g"]],
                                                   axis=0), state[hl]) for hl in range(n_local)]

            def correct(c=c, tmp=tmp):
                tmp["u"] = [st["sols"][(c, hl)][:, :DV] - tmp["ws"][hl][:c_len] for hl in range(n_local)]
                tmp["out"] = [tmp["ws"][hl][c_len:] + _bdot(st["prep"][(c, hl)]["qk"], tmp["u"][hl])
                              for hl in range(n_local)]

            def advance(c=c, tmp=tmp):
                rs = slice(c * c_len, (c + 1) * c_len)
                for hl in range(n_local):
                    p = st["prep"][(c, hl)]
                    g_last = p["gcum"][c_len - 1:c_len, :]
                    k_tail = p["kf"] * jnp.exp(g_last - p["gcum"])
                    state[hl] = (state[hl] * jnp.exp(g_last)
                                 + _dot_tn(k_tail.astype(BF16), tmp["u"][hl].astype(BF16)))
                for hl in range(n_local):
                    hs = slice(hl * DV, (hl + 1) * DV)
                    zf = z_ref[rs, hs].astype(F32)
                    o_ref[rs, hs] = (_rms(tmp["out"][hl], nw) * _silu(zf)).astype(o_ref.dtype)

            stages += [apply_state, correct, advance]
        return stages

    half = max(n_chunks // 2, 1)
    st_a, prep_a = prep_stages(range(half))
    st_b, prep_b = prep_stages(range(half, n_chunks))
    for stage in prep_a:
        stage()
    chain_a = chain_stages(range(half), st_a)
    fill = iter(prep_b if half < n_chunks else [])
    per = -(-len(prep_b) // len(chain_a))
    for stage in chain_a:
        stage()
        for _ in range(per):
            nxt = next(fill, None)
            if nxt is not None:
                nxt()
    for nxt in fill:
        nxt()
    if half < n_chunks:
        for stage in chain_stages(range(half, n_chunks), st_b):
            stage()
    for hl in range(n_local):
        s_ref[0, hl] = state[hl]


def gated_delta_prompt(qkvc, z, gb, gt, norm_w, n_seq, seq_len, key_dim, n_heads):
    n_qk = key_dim // DK
    hps = n_heads // n_qk
    qps = 4 if n_qk % 4 == 0 else 1
    tc = _pick(seq_len, 256)
    nt = seq_len // tc
    qw, vw = qps * DK, qps * hps * DV
    k_blk0, v_blk0 = key_dim // qw, 2 * key_dim // vw
    rb = lambda b, t: b * nt + t
    return pl.pallas_call(
        functools.partial(_delta_kernel, n_chunks=tc // CHUNK_B, n_heads=n_heads, heads_per_qk=hps,
                          qk_per_step=qps),
        out_shape=(jax.ShapeDtypeStruct((z.shape[0], n_heads * DV), BF16),
                   jax.ShapeDtypeStruct((n_seq, n_heads, DK, DV), F32)),
        grid=(n_seq, n_qk // qps, nt),
        in_specs=[pl.BlockSpec((tc, qw), lambda b, h, t: (rb(b, t), h)),
                  pl.BlockSpec((tc, qw), lambda b, h, t: (rb(b, t), k_blk0 + h)),
                  pl.BlockSpec((tc, vw), lambda b, h, t: (rb(b, t), v_blk0 + h)),
                  pl.BlockSpec((tc, vw), lambda b, h, t: (rb(b, t), h)),
                  pl.BlockSpec((tc, 2 * n_heads), lambda b, h, t: (rb(b, t), 0)),
                  pl.BlockSpec((n_heads, tc), lambda b, h, t: (0, rb(b, t))),
                  pl.BlockSpec((1, DV), lambda b, h, t: (0, 0)),
                  pl.BlockSpec(memory_space=pl.ANY)],
        out_specs=(pl.BlockSpec((tc, vw), lambda b, h, t: (rb(b, t), h)),
                   pl.BlockSpec((1, qps * hps, DK, DV), lambda b, h, t: (b, h, 0, 0))),
        input_output_aliases={7: 0},
        compiler_params=_params(("arbitrary", "arbitrary", "arbitrary"), 32),
        name="gated_delta_prompt",
    )(qkvc, qkvc, qkvc, z, gb, gt, norm_w.reshape(1, DV), jnp.zeros((z.shape[0], n_heads * DV), BF16))


def _delta_sample_kernel(q_ref, k_ref, v_ref, z_ref, gb_ref, gt_ref, nw_ref, s0_ref, *rest,
                         seq_len, n_heads, heads_per_qk, slab):
    o_ref, s_ref = rest[-2:]
    if slab is not None:
        for other in range(s_ref.shape[0]):
            if other != slab:
                s_ref[other] = jnp.zeros(s_ref.shape[1:], F32)
        s_ref = s_ref.at[slab]
    c_len = CHUNK_B
    hpb = pl.program_id(1)
    hps = heads_per_qk
    rows = q_ref.shape[0]
    per = c_len // seq_len
    masks = _chunk_masks(c_len, seq_len)
    nw = nw_ref[...]

    prep = []
    for c in range(rows // c_len):
        rs = slice(c * c_len, (c + 1) * c_len)
        kc = k_ref[rs, :]
        qc = q_ref[rs, :]
        kk_w = _dot_nt(kc, jnp.concatenate([kc, jnp.zeros_like(kc)], axis=0))
        qkm = _dot_nt(qc, kc)
        kf = kc.astype(F32)
        qf = qc.astype(F32)
        gb = gb_ref[rs, :]
        gt = gt_ref[:, rs]
        for hh in range(hps):
            vf = v_ref[rs, hh * DV:(hh + 1) * DV].astype(F32)
            prep.append(_chunk_setup(kk_w, qkm, kf, qf, vf, gb, gt, hpb * hps + hh, n_heads, masks))
    ws_ = [p["w0"] for p in prep]
    for _ in range(seq_len.bit_length() - 1):
        ws_ = [_product_step(w, masks["left_w"]) for w in ws_]
    sols = [_apply_inverse(w, p["rhs"]) for w, p in zip(ws_, prep)]

    for c in range(rows // c_len):
        rs = slice(c * c_len, (c + 1) * c_len)
        for hh in range(hps):
            p, sol = prep[c * hps + hh], sols[c * hps + hh]
            u_v, w_k = sol[:, :DV], sol[:, DV:]
            k_tail = p["kf"] * jnp.exp(p["gsum"] - p["gcum"])
            g_tail = jnp.exp(p["gsum"])
            s_old = [s0_ref[c * per + i, hh] for i in range(per)]
            seq = [slice(i * seq_len, (i + 1) * seq_len) for i in range(per)]
            ws = [_bdot(jnp.concatenate([w_k[r], p["q_g"][r]], axis=0), s) for r, s in zip(seq, s_old)]
            u_new = jnp.concatenate([u_v[r] - w[:seq_len] for r, w in zip(seq, ws)], axis=0)
            out = jnp.concatenate([w[seq_len:] for w in ws], axis=0) + _bdot(p["qk"], u_new)
            for i, r in enumerate(seq):
                s_ref[c * per + i, hh] = (s_old[i] * g_tail[i * seq_len:i * seq_len + 1, :]
                                          + _dot_tn(k_tail[r].astype(BF16), u_new[r].astype(BF16)))
            hs = slice(hh * DV, (hh + 1) * DV)
            zf = z_ref[rs, hs].astype(F32)
            o_ref[rs, hs] = (_rms(out, nw) * _silu(zf)).astype(o_ref.dtype)


def gated_delta_sample(qkvc, z, gb, gt, norm_w, s0, layer, row0, n_seq, seq_len, key_dim, o_all, s_all):
    n_heads = s0.shape[2]
    n_qk = key_dim // DK
    hps = n_heads // n_qk
    tr = LANES
    sps = tr // seq_len
    assert row0 % tr == 0 and n_seq % sps == 0 and seq_len == SUBLANES
    rb0 = row0 // tr
    vw = hps * DV
    v_blk0 = 2 * key_dim // vw
    args = [qkvc, qkvc, qkvc, z, gb, gt, norm_w.reshape(1, DV), s0, o_all]
    in_specs = [pl.BlockSpec((tr, DK), lambda g, h: (g, h)),
                pl.BlockSpec((tr, DK), lambda g, h: (g, n_qk + h)),
                pl.BlockSpec((tr, vw), lambda g, h: (g, v_blk0 + h)),
                pl.BlockSpec((tr, vw), lambda g, h: (rb0 + g, h)),
                pl.BlockSpec((tr, 2 * n_heads), lambda g, h: (rb0 + g, 0)),
                pl.BlockSpec((n_heads, tr), lambda g, h: (0, rb0 + g)),
                pl.BlockSpec((1, DV), lambda g, h: (0, 0)),
                pl.BlockSpec((None, sps, hps, DK, DV), lambda g, h: (layer, g, h, 0, 0)),
                pl.BlockSpec(memory_space=pl.ANY)]
    aliases = {8: 0}
    if s_all is not None:
        args.append(s_all)
        in_specs.append(pl.BlockSpec(memory_space=pl.ANY))
        aliases[9] = 1
        slab = None
        s_spec = pl.BlockSpec((None, sps, hps, DK, DV), lambda g, h: (layer, g, h, 0, 0))
    else:
        slab = layer
        s_spec = pl.BlockSpec((s0.shape[0], sps, hps, DK, DV), lambda g, h: (0, g, h, 0, 0))
    return pl.pallas_call(
        functools.partial(_delta_sample_kernel, seq_len=seq_len, n_heads=n_heads, heads_per_qk=hps, slab=slab),
        out_shape=(jax.ShapeDtypeStruct(o_all.shape, BF16), jax.ShapeDtypeStruct(s0.shape, F32)),
        grid=(n_seq // sps, n_qk),
        in_specs=in_specs,
        out_specs=(pl.BlockSpec((tr, vw), lambda g, h: (rb0 + g, h)), s_spec),
        input_output_aliases=aliases,
        compiler_params=_params(("arbitrary", "arbitrary"), 32),
        name="gated_delta_sample",
    )(*args)


def _router_kernel(x_ref, nw_ref, wr_ref, br_ref, h_ref, sel_ref, gate_ref, cnt_ref, cnt_sc):
    tm = x_ref.shape[0]

    @pl.when(pl.program_id(0) == 0)
    def _():
        cnt_sc[...] = jnp.zeros_like(cnt_sc)

    h = _rms(x_ref[...], nw_ref[...])
    h_ref[...] = h
    h_hi = h.astype(BF16)
    h_lo = (h - h_hi.astype(F32)).astype(BF16)
    h3 = jnp.concatenate([h_hi, h_hi, h_lo], axis=1)
    logits = jnp.dot(h3, wr_ref[...], preferred_element_type=F32) + br_ref[...]
    lane = lax.broadcasted_iota(I32, (tm, LANES), 1)
    lane_f = lane.astype(F32)
    big = float(LANES)

    is_g = lane < N_GROUPS
    gl = jnp.where(is_g, logits, -jnp.inf)
    ge = jnp.exp(gl - jnp.max(gl, axis=-1, keepdims=True))
    gp = ge / jnp.sum(ge, axis=-1, keepdims=True)
    p_group = jnp.max(gp, axis=-1, keepdims=True)
    g_sel = jnp.min(jnp.where(is_g & (gp == p_group), lane_f, big), axis=-1, keepdims=True)

    e_lane = lane - N_GROUPS
    in_grp = (e_lane >= 0) & (e_lane < N_EXPERTS) & ((e_lane >> 3).astype(F32) == g_sel)
    el = jnp.where(in_grp, logits, -jnp.inf)
    ee = jnp.exp(el - jnp.max(el, axis=-1, keepdims=True))
    ep = ee / jnp.sum(ee, axis=-1, keepdims=True)
    p1 = jnp.max(ep, axis=-1, keepdims=True)
    i1 = jnp.min(jnp.where(in_grp & (ep == p1), lane_f, big), axis=-1, keepdims=True)
    rest = in_grp & (lane_f != i1)
    p2 = jnp.max(jnp.where(rest, ep, -1.0), axis=-1, keepdims=True)
    i2 = jnp.min(jnp.where(rest & (ep == p2), lane_f, big), axis=-1, keepdims=True)
    denom = p1 + p2
    gate1 = p_group * p1 / denom
    gate2 = p_group * p2 / denom
    e1 = i1 - float(N_GROUPS)
    e2 = i2 - float(N_GROUPS)

    onehot = (lane_f == e1) | (lane_f == e2)
    oh = jnp.where(onehot, 1.0, 0.0)
    rr = lax.broadcasted_iota(I32, (tm, tm), 0)
    cc = lax.broadcasted_iota(I32, (tm, tm), 1)
    tri = jnp.where(cc < rr, 1.0, 0.0).astype(BF16)
    before = jnp.dot(tri, oh.astype(BF16), preferred_element_type=F32) + cnt_sc[...]
    r1 = jnp.sum(jnp.where(lane_f == e1, before, 0.0), axis=-1, keepdims=True)
    r2 = jnp.sum(jnp.where(lane_f == e2, before, 0.0), axis=-1, keepdims=True)
    cnt_sc[...] = cnt_sc[...] + jnp.sum(oh, axis=0, keepdims=True)
    cnt_ref[...] = cnt_sc[...]

    sel = jnp.where(lane == 0, e1, jnp.where(lane == 1, e2, jnp.where(lane == 2, r1, jnp.where(lane == 3, r2, 0.0))))
    sel_ref[...] = sel.astype(I32)
    gate_ref[...] = jnp.where(lane == 0, gate1, jnp.where(lane == 1, gate2, 0.0))


def moe_router(x, norm_w, w_route, b_route):
    n, d = x.shape
    tm = _pick(n, 256)
    return pl.pallas_call(
        _router_kernel,
        out_shape=(jax.ShapeDtypeStruct((n, d), F32),
                   jax.ShapeDtypeStruct((n, LANES), I32),
                   jax.ShapeDtypeStruct((n, LANES), F32),
                   jax.ShapeDtypeStruct((1, LANES), F32)),
        grid=(n // tm,),
        in_specs=[pl.BlockSpec((tm, d), lambda i: (i, 0)),
                  pl.BlockSpec((1, d), lambda i: (0, 0)),
                  pl.BlockSpec((3 * d, LANES), lambda i: (0, 0)),
                  pl.BlockSpec((1, LANES), lambda i: (0, 0))],
        out_specs=(pl.BlockSpec((tm, d), lambda i: (i, 0)),
                   pl.BlockSpec((tm, LANES), lambda i: (i, 0)),
                   pl.BlockSpec((tm, LANES), lambda i: (i, 0)),
                   pl.BlockSpec((1, LANES), lambda i: (0, 0))),
        scratch_shapes=[pltpu.VMEM((1, LANES), F32)],
        compiler_params=_params(("arbitrary",), 32),
        name="moe_router",
    )(x, norm_w.reshape(1, d), w_route, b_route)


ISSUE_UNROLL = 8


def _slot(slot_ref, r, k):
    return slot_ref[0, 0, TOP_K * r + k]


def _row_copy_out(h_ref, xs_ref, sem, r, d):
    return pltpu.make_async_copy(h_ref.at[pl.ds(r, 1), :], xs_ref.at[pl.ds(d, 1), :], sem)


def _dispatch_kernel(slot_ref, h_ref, xs_in_ref, xs_ref, sem):
    del xs_in_ref
    tm = h_ref.shape[0]

    def issue(r, carry):
        for k in range(TOP_K):
            _row_copy_out(h_ref, xs_ref, sem, r, _slot(slot_ref, r, k)).start()
        return carry

    lax.fori_loop(0, tm, issue, 0, unroll=ISSUE_UNROLL)
    for k in range(TOP_K):
        pltpu.make_async_copy(h_ref, xs_ref.at[pl.ds(0, tm), :], sem).wait()


def moe_dispatch(h, slot3, n_slots):
    n, d = h.shape
    tm = slot3.shape[2] // TOP_K
    xs0 = jnp.zeros((n_slots, d), F32)
    return pl.pallas_call(
        _dispatch_kernel,
        out_shape=jax.ShapeDtypeStruct((n_slots, d), F32),
        grid=(n // tm,),
        in_specs=[pl.BlockSpec((1, 1, TOP_K * tm), lambda i: (i, 0, 0), memory_space=pltpu.SMEM),
                  pl.BlockSpec((tm, d), lambda i: (i, 0)),
                  pl.BlockSpec(memory_space=pl.ANY)],
        out_specs=pl.BlockSpec(memory_space=pl.ANY),
        scratch_shapes=[pltpu.SemaphoreType.DMA(())],
        input_output_aliases={2: 0},
        compiler_params=_params(("arbitrary",), 32),
        name="moe_dispatch",
    )(slot3, h, xs0)


WEIGHT_SLOTS = 2


def _ffn_kernel(bseq_ref, first_ref, sexp_ref, meta_ref, x_ref, wg_hbm, wu_hbm, wd_hbm, o_ref,
                wgf, wuf, wdf, wgb, wub, wdb, sem, *, layer):
    b = pl.program_id(0)
    n_used, n_seq = meta_ref[0], meta_ref[1]
    valid = b < n_used
    j = bseq_ref[b]
    slot = lax.rem(j, WEIGHT_SLOTS)

    def fetch(jj, sl):
        e = sexp_ref[jj]
        return [pltpu.make_async_copy(hbm.at[layer, e], buf.at[sl], sem.at[sl, i])
                for i, (hbm, buf) in enumerate(((wg_hbm, wgf), (wu_hbm, wuf), (wd_hbm, wdf)))]

    for ahead in range(WEIGHT_SLOTS):
        @pl.when((b == 0) & (ahead < n_seq))
        def _(ahead=ahead):
            for cp in fetch(ahead, ahead):
                cp.start()

    @pl.when(valid & (first_ref[b] == 1))
    def _():
        for cp in fetch(j, slot):
            cp.wait()
        wgb[...] = wgf[slot].astype(BF16)
        wub[...] = wuf[slot].astype(BF16)
        wdb[...] = wdf[slot].astype(BF16)

        @pl.when(j + WEIGHT_SLOTS < n_seq)
        def _():
            for cp in fetch(j + WEIGHT_SLOTS, slot):
                cp.start()

    @pl.when(valid)
    def _():
        x = x_ref[...].astype(BF16)
        gate = jnp.dot(x, wgb[...], preferred_element_type=F32)
        up = jnp.dot(x, wub[...], preferred_element_type=F32)
        hid = (_silu(gate) * up).astype(BF16)
        o_ref[...] = jnp.dot(hid, wdb[...], preferred_element_type=F32)

    @pl.when(jnp.logical_not(valid))
    def _():
        o_ref[...] = jnp.zeros_like(o_ref)


def moe_experts(xs, block_seq, block_first, seq_expert, meta, w_gate, w_up, w_down, layer):
    n_slots, d = xs.shape
    de = w_gate.shape[-1]
    n_blocks = n_slots // MOE_ROWS
    xmap = lambda b, bs, fi, se, me: (jnp.minimum(b, jnp.maximum(me[0] - 1, 0)), 0)
    grid_spec = pltpu.PrefetchScalarGridSpec(
        num_scalar_prefetch=4,
        grid=(n_blocks,),
        in_specs=[pl.BlockSpec((MOE_ROWS, d), xmap),
                  pl.BlockSpec(memory_space=pl.ANY),
                  pl.BlockSpec(memory_space=pl.ANY),
                  pl.BlockSpec(memory_space=pl.ANY)],
        out_specs=pl.BlockSpec((MOE_ROWS, d), lambda b, bs, fi, se, me: (b, 0)),
        scratch_shapes=[pltpu.VMEM((WEIGHT_SLOTS, d, de), F32), pltpu.VMEM((WEIGHT_SLOTS, d, de), F32),
                        pltpu.VMEM((WEIGHT_SLOTS, de, d), F32),
                        pltpu.VMEM((d, de), BF16), pltpu.VMEM((d, de), BF16), pltpu.VMEM((de, d), BF16),
                        pltpu.SemaphoreType.DMA((WEIGHT_SLOTS, 3))],
    )
    return pl.pallas_call(
        functools.partial(_ffn_kernel, layer=layer),
        out_shape=jax.ShapeDtypeStruct((n_slots, d), F32),
        grid_spec=grid_spec,
        compiler_params=_params(("arbitrary",), 56),
        name="moe_experts",
    )(block_seq, block_first, seq_expert, meta, xs, w_gate, w_up, w_down)


def _row_copy_in(yb_ref, buf_ref, sem, k, r, d):
    return pltpu.make_async_copy(yb_ref.at[pl.ds(d, 1), :], buf_ref.at[k, pl.ds(r, 1), :], sem)


def _combine_kernel(slot_ref, x_ref, gate_ref, nw_ref, yb_ref, o1_ref, o2_ref, buf_ref, sem, *, split_step):
    tm = x_ref.shape[0]

    def issue(r, carry):
        for k in range(TOP_K):
            _row_copy_in(yb_ref, buf_ref, sem, k, r, _slot(slot_ref, r, k)).start()
        return carry

    lax.fori_loop(0, tm, issue, 0, unroll=ISSUE_UNROLL)
    for k in range(TOP_K):
        pltpu.make_async_copy(yb_ref.at[pl.ds(0, tm), :], buf_ref.at[k], sem).wait()

    g = gate_ref[...]
    y = buf_ref[0] * g[:, 0:1] + buf_ref[1] * g[:, 1:2]
    xn = x_ref[...] + y
    hn = _rms(xn, nw_ref[...])
    if split_step is None:
        o1_ref[...] = xn
        o2_ref[...] = hn.astype(o2_ref.dtype)
    else:
        @pl.when(pl.program_id(0) < split_step)
        def _():
            o1_ref[...] = hn

        @pl.when(pl.program_id(0) >= split_step)
        def _():
            o2_ref[...] = hn


def moe_combine(x, yb, slot3, gates, next_norm_w, split_rows=None):
    n, d = x.shape
    tm = slot3.shape[2] // TOP_K
    if split_rows is None:
        split_step = None
        out_shape = (jax.ShapeDtypeStruct((n, d), F32), jax.ShapeDtypeStruct((n, d), BF16))
        out_specs = (pl.BlockSpec((tm, d), lambda i: (i, 0)), pl.BlockSpec((tm, d), lambda i: (i, 0)))
    else:
        assert split_rows % tm == 0 and 0 < split_rows < n
        split_step = split_rows // tm
        out_shape = (jax.ShapeDtypeStruct((split_rows, d), F32), jax.ShapeDtypeStruct((n - split_rows, d), F32))
        out_specs = (pl.BlockSpec((tm, d), lambda i: (jnp.minimum(i, split_step - 1), 0)),
                     pl.BlockSpec((tm, d), lambda i: (jnp.maximum(i - split_step, 0), 0)))
    return pl.pallas_call(
        functools.partial(_combine_kernel, split_step=split_step),
        out_shape=out_shape,
        grid=(n // tm,),
        in_specs=[pl.BlockSpec((1, 1, TOP_K * tm), lambda i: (i, 0, 0), memory_space=pltpu.SMEM),
                  pl.BlockSpec((tm, d), lambda i: (i, 0)),
                  pl.BlockSpec((tm, LANES), lambda i: (i, 0)),
                  pl.BlockSpec((1, d), lambda i: (0, 0)),
                  pl.BlockSpec(memory_space=pl.ANY)],
        out_specs=out_specs,
        scratch_shapes=[pltpu.VMEM((TOP_K, tm, d), F32), pltpu.SemaphoreType.DMA(())],
        compiler_params=_params(("arbitrary",), 40),
        name="moe_combine",
    )(slot3, x, gates, next_norm_w.reshape(1, d), yb)


def hier_moe(x, ffn_norm_w, w_route, b_route, w_gate, w_up, w_down, layer, next_norm_w, split_rows=None):
    n, _ = x.shape
    h, sel, gates, cnt = moe_router(x, ffn_norm_w, w_route, b_route)
    counts = cnt[0, :N_EXPERTS].astype(I32)
    padded = (counts + MOE_ROWS - 1) // MOE_ROWS * MOE_ROWS
    pend = jnp.cumsum(padded)
    pstart = (pend - padded).astype(I32)
    tm = _pick(n, 256)
    expert, rank = sel[:, :TOP_K], sel[:, TOP_K:2 * TOP_K]
    is_e = expert[:, :, None] == jnp.arange(N_EXPERTS, dtype=I32)
    slot3 = (rank + jnp.sum(jnp.where(is_e, pstart, 0), axis=-1)).reshape(n // tm, 1, TOP_K * tm)
    n_blocks = -(-(n * TOP_K) // MOE_ROWS) + N_EXPERTS
    n_used = pend[-1] // MOE_ROWS
    blk = jnp.arange(n_blocks, dtype=I32)
    last = jnp.maximum(n_used - 1, 0)
    bstart = jnp.minimum(blk, last) * MOE_ROWS
    block_expert = jnp.minimum(jnp.sum(pend[None, :] <= bstart[:, None], axis=1), N_EXPERTS - 1).astype(I32)
    block_first = (bstart == pstart[block_expert]).astype(I32)
    used = counts > 0
    pos = jnp.cumsum(used.astype(I32)) - 1
    ids = jnp.arange(N_EXPERTS, dtype=I32)
    seq_expert = jnp.sum(jnp.where(used[None, :] & (pos[None, :] == ids[:, None]), ids[None, :], 0), axis=1)
    meta = jnp.stack([n_used, jnp.sum(used.astype(I32))]).astype(I32)
    xs = moe_dispatch(h, slot3, n_blocks * MOE_ROWS)
    yb = moe_experts(xs, pos[block_expert], block_first, seq_expert.astype(I32), meta, w_gate, w_up, w_down, layer)
    return moe_combine(x, yb, slot3, gates, next_norm_w, split_rows)


def kernel(x_prompt, x_sample, state_delta_S, state_delta_conv, norm_mixer_w, norm_ffn_w, norm_final_w,
           a_w_in, a_ln_g, a_ln_b, a_w_s, a_b_s, a_w_out,
           b_w_in, b_conv_w, b_a_log, b_dt_bias, b_norm_w, b_w_out,
           moe_w_group, moe_b_group, moe_w_router, moe_b_router, moe_w_gate, moe_w_up, moe_w_down):
    nb, seq, d = x_prompt.shape
    db, dseq, _ = x_sample.shape
    depth = norm_mixer_w.shape[0]
    n_p, n_s = nb * seq, db * dseq
    n = n_p + n_s
    a_width = a_ln_g.shape[1]
    n_groups_a = a_width // A_GROUP_DIM
    n_heads = b_a_log.shape[1]
    value_dim = n_heads * DV
    conv_dim = b_conv_w.shape[2]
    key_dim = (conv_dim - value_dim) // 2
    assert seq % CHUNK_A == 0 and n_s % CHUNK_A == 0 and CHUNK_A % dseq == 0 and dseq == SUBLANES
    assert seq % CHUNK_B == 0 and dseq <= CHUNK_B and dseq >= CONV_W - 1


    causal = jnp.tril(jnp.ones((CHUNK_A, CHUNK_A), bool))
    ws_p = jnp.where(causal, a_w_s, 0.0)
    per = CHUNK_A // dseq
    ws_s = jnp.where(causal[:dseq, :dseq], a_w_s[:, :, :dseq, :dseq], 0.0)
    ws_s = jnp.einsum("ij,lgts->lgitjs", jnp.eye(per, dtype=F32), ws_s).reshape(a_w_s.shape)
    ws2 = jnp.stack([ws_p, ws_s], axis=1).astype(BF16)
    bias_p = jnp.repeat(jnp.swapaxes(a_b_s, 1, 2), A_GROUP_DIM, axis=2)
    bias_s = jnp.tile(bias_p[:, :dseq], (1, per, 1))
    bias2 = jnp.stack([bias_p, bias_s], axis=1)

    w_route = jnp.concatenate([moe_w_group, moe_w_router,
                               jnp.zeros((depth, d, LANES - N_GROUPS - N_EXPERTS), F32)], axis=2)
    w_route_hi = w_route.astype(BF16)
    w_route_lo = (w_route - w_route_hi.astype(F32)).astype(BF16)
    w_route = jnp.concatenate([w_route_hi, w_route_lo, w_route_hi], axis=1)
    b_route = jnp.concatenate([moe_b_group, moe_b_router,
                               jnp.zeros((depth, LANES - N_GROUPS - N_EXPERTS), F32)], axis=1)
    b_w_in_t = jnp.swapaxes(b_w_in, 1, 2)
    zeros_h = jnp.zeros_like(b_a_log)
    a_log2 = jnp.concatenate([zeros_h, b_a_log], axis=1)
    dt2 = jnp.concatenate([zeros_h, b_dt_bias], axis=1)

    chunk_v, s_prompt, conv_prompt, conv_sample = [], [], [], []
    s_sample = None
    x, h = concat_rmsnorm(x_prompt.reshape(n_p, d), x_sample.reshape(n_s, d), norm_mixer_w[0])
    for i in range(depth):
        j = i // 2
        if i % 2 == 0:
            u = matmul(h, a_w_in, j, 0, a_width, BF16, act="gelu")
            vpre = matmul(h, a_w_in, j, a_width, a_width, F32, act="gelu")
            gated, v_s = gmlp_gate(u, vpre, a_ln_g[j], a_ln_b[j], ws2[j], bias2[j], n_p // CHUNK_A)
            chunk_v.append(v_s.reshape(db, dseq, a_width))
            x = matmul(gated, a_w_out, j, 0, d, F32, res=x)
        else:
            qkv = matmul(h, b_w_in_t, j, 0, conv_dim, F32, w_t=True)
            z = matmul(h, b_w_in_t, j, conv_dim, value_dim, BF16, w_t=True)
            gb = matmul(h, b_w_in_t, j, conv_dim + value_dim, 2 * n_heads, F32, act="gdn_gates",
                        extra=(a_log2[j:j + 1], dt2[j:j + 1]), w_t=True)
            gt = gb[:, n_heads:].T
            keep = CONV_W - 1
            conv_prompt.append(jnp.stack([qkv[(b + 1) * seq - keep:(b + 1) * seq] for b in range(nb)]))
            conv_sample.append(qkv[n_p:].reshape(db, dseq, conv_dim)[:, dseq - keep:])
            pad8 = ((0, 0), (SUBLANES - keep, 0), (0, 0))
            st8_p = jnp.zeros((nb * SUBLANES, conv_dim), F32)
            st8_s = jnp.pad(state_delta_conv[j], pad8).reshape(db * SUBLANES, conv_dim)
            qc_p = gdn_conv_prompt(qkv, st8_p, b_conv_w, j, nb, seq, key_dim)
            qc_s = gdn_conv_sample(qkv, n_p, st8_s, b_conv_w, j, n_s, dseq, key_dim)
            o_p, s_p = gated_delta_prompt(qc_p, z, gb, gt, b_norm_w[j], nb, seq, key_dim, n_heads)
            gated, s_sample = gated_delta_sample(qc_s, z, gb, gt, b_norm_w[j], state_delta_S, j, n_p, db, dseq,
                                                 key_dim, o_p, s_sample)
            s_prompt.append(s_p)
            x = matmul(gated, b_w_out, j, 0, d, F32, res=x)
        moe_w = (norm_ffn_w[i], w_route[i], b_route[i:i + 1], moe_w_gate, moe_w_up, moe_w_down, i)
        if i < depth - 1:
            x, h = hier_moe(x, *moe_w, norm_mixer_w[i + 1])
        else:
            y_p, y_s = hier_moe(x, *moe_w, norm_final_w, split_rows=n_p)

    y_prompt = y_p.reshape(nb, seq, d)
    y_sample = y_s.reshape(db, dseq, d)
    return (y_prompt, y_sample, jnp.stack(chunk_v), jnp.stack(s_prompt), jnp.stack(conv_prompt),
            s_sample, jnp.stack(conv_sample))
```

```python
import functools
import math

import jax
import jax.numpy as jnp
from jax import lax
from jax.experimental import pallas as pl
from jax.experimental.pallas import tpu as pltpu

F32 = jnp.float32
BF16 = jnp.bfloat16
I32 = jnp.int32

EPS = 1e-6
LANES = 128
SUBLANES = 8
CHUNK_A = 128
A_GROUP_DIM = 128
DK = 128
DV = 128
CONV_W = 4
CHUNK_B = 64
N_GROUPS = 8
EXPERTS_PER_GROUP = 8
N_EXPERTS = N_GROUPS * EXPERTS_PER_GROUP
TOP_K = 2
MOE_ROWS = 256
MIB = 2 ** 20


def _params(semantics, vmem_mib):
    return pltpu.CompilerParams(dimension_semantics=semantics, vmem_limit_bytes=vmem_mib * MIB)


def _pick(n, pref):
    t = min(n, pref)
    while n % t:
        t -= SUBLANES
    return t


def _rms(x, w):
    return x * lax.rsqrt(jnp.mean(x * x, axis=-1, keepdims=True) + EPS) * w


def _silu(x):
    return x * jax.nn.sigmoid(x)


def _concat_rmsnorm_kernel(xa_ref, xb_ref, w_ref, x_ref, h_ref, *, steps_a):
    @pl.when(pl.program_id(0) < steps_a)
    def _():
        x_ref[...] = xa_ref[...]

    @pl.when(pl.program_id(0) >= steps_a)
    def _():
        x_ref[...] = xb_ref[...]

    h_ref[...] = _rms(x_ref[...], w_ref[...]).astype(h_ref.dtype)


def concat_rmsnorm(xa, xb, w):
    (na, d), (nb_, _) = xa.shape, xb.shape
    tm = _pick(math.gcd(na, nb_), 512)
    steps_a = na // tm
    return pl.pallas_call(
        functools.partial(_concat_rmsnorm_kernel, steps_a=steps_a),
        out_shape=(jax.ShapeDtypeStruct((na + nb_, d), F32), jax.ShapeDtypeStruct((na + nb_, d), BF16)),
        grid=((na + nb_) // tm,),
        in_specs=[pl.BlockSpec((tm, d), lambda i: (jnp.minimum(i, steps_a - 1), 0)),
                  pl.BlockSpec((tm, d), lambda i: (jnp.maximum(i - steps_a, 0), 0)),
                  pl.BlockSpec((1, d), lambda i: (0, 0))],
        out_specs=(pl.BlockSpec((tm, d), lambda i: (i, 0)), pl.BlockSpec((tm, d), lambda i: (i, 0))),
        compiler_params=_params(("arbitrary",), 32),
        name="concat_rmsnorm",
    )(xa, xb, w.reshape(1, d))


def _mm_kernel(*refs, act, has_res, n_extra, w_t):
    a_ref, w_ref = refs[:2]
    extra = refs[2:2 + n_extra]
    rest = refs[2 + n_extra:]
    if has_res:
        r_ref, o_ref, wb_ref = rest
    else:
        o_ref, wb_ref = rest

    @pl.when(pl.program_id(1) == 0)
    def _():
        wb_ref[...] = w_ref[...].astype(BF16)

    if w_t:
        acc = _dot_nt(a_ref[...], wb_ref[...])
    else:
        acc = jnp.dot(a_ref[...], wb_ref[...], preferred_element_type=F32)
    if act == "gelu":
        acc = jax.nn.gelu(acc)
    elif act == "gdn_gates":
        a_log_ref, dt_ref = extra
        h = acc.shape[1] // 2
        lane = lax.broadcasted_iota(I32, acc.shape, 1)
        g = -jnp.exp(a_log_ref[...]) * jax.nn.softplus(acc + dt_ref[...])
        acc = jnp.where(lane < h, jax.nn.sigmoid(acc), g)
    if has_res:
        acc = acc + r_ref[...]
    o_ref[...] = acc.astype(o_ref.dtype)


def matmul(a, w, layer, col0, ncols, out_dtype, *, act=None, res=None, extra=(), tm=1024, tn=512, w_t=False):
    m, k = a.shape
    tm = _pick(m, tm)
    if k <= 2048 and ncols % (2 * tn) == 0 and col0 % (2 * tn) == 0:
        tn = 2 * tn
    tn = min(tn, ncols)
    assert ncols % tn == 0 and col0 % tn == 0
    c0 = col0 // tn
    if w_t:
        w_spec = pl.BlockSpec((None, tn, k), lambda j, i: (layer, c0 + j, 0))
    else:
        w_spec = pl.BlockSpec((None, k, tn), lambda j, i: (layer, 0, c0 + j))
    in_specs = [pl.BlockSpec((tm, k), lambda j, i: (i, 0)), w_spec]
    args = [a, w]
    for e in extra:
        in_specs.append(pl.BlockSpec((1, tn), lambda j, i: (0, j)))
        args.append(e)
    if res is not None:
        in_specs.append(pl.BlockSpec((tm, tn), lambda j, i: (i, j)))
        args.append(res)
    out_bytes = jnp.dtype(out_dtype).itemsize
    vmem = (2 * tm * k * 2 + 2 * k * tn * 4 + k * tn * 2 + 2 * tm * tn * out_bytes
            + (2 * tm * tn * 4 if res is not None else 0) + 3 * tm * tn * 4)
    return pl.pallas_call(
        functools.partial(_mm_kernel, act=act, has_res=res is not None, n_extra=len(extra), w_t=w_t),
        out_shape=jax.ShapeDtypeStruct((m, ncols), out_dtype),
        grid=(ncols // tn, m // tm),
        in_specs=in_specs,
        out_specs=pl.BlockSpec((tm, tn), lambda j, i: (i, j)),
        scratch_shapes=[pltpu.VMEM((tn, k) if w_t else (k, tn), BF16)],
        compiler_params=_params(("arbitrary", "arbitrary"), min(56, vmem // MIB + 8)),
        name="matmul_" + (act or "plain") + ("_res" if res is not None else ""),
    )(*args)


def _gmlp_gate_kernel(u_ref, v_ref, g_ref, b_ref, ws_ref, bias_ref, o_ref, vo_ref, *, n_groups, first_sample):
    v = v_ref[...]
    xc = v - jnp.mean(v, axis=-1, keepdims=True)
    var = jnp.mean(xc * xc, axis=-1, keepdims=True)
    vn = xc * lax.rsqrt(var + EPS) * g_ref[...] + b_ref[...]

    @pl.when(pl.program_id(0) >= first_sample)
    def _():
        vo_ref[...] = vn

    vb = vn.astype(BF16)
    for g in range(n_groups):
        sl = slice(g * A_GROUP_DIM, (g + 1) * A_GROUP_DIM)
        s = jnp.dot(ws_ref[g], vb[:, sl], preferred_element_type=F32) + bias_ref[:, sl]
        o_ref[:, sl] = (u_ref[:, sl].astype(F32) * s).astype(BF16)


def gmlp_gate(u, vpre, ln_g, ln_b, ws2, bias2, n_prompt_chunks):
    n, aw = u.shape
    n_chunks = n // CHUNK_A
    n_groups = aw // A_GROUP_DIM
    n_sample_rows = n - n_prompt_chunks * CHUNK_A
    which = lambda c: jnp.where(c >= n_prompt_chunks, 1, 0)
    return pl.pallas_call(
        functools.partial(_gmlp_gate_kernel, n_groups=n_groups, first_sample=n_prompt_chunks),
        out_shape=(jax.ShapeDtypeStruct((n, aw), BF16),
                   jax.ShapeDtypeStruct((n_sample_rows, aw), F32)),
        grid=(n_chunks,),
        in_specs=[pl.BlockSpec((CHUNK_A, aw), lambda c: (c, 0)),
                  pl.BlockSpec((CHUNK_A, aw), lambda c: (c, 0)),
                  pl.BlockSpec((1, aw), lambda c: (0, 0)),
                  pl.BlockSpec((1, aw), lambda c: (0, 0)),
                  pl.BlockSpec((None, n_groups, CHUNK_A, CHUNK_A), lambda c: (which(c), 0, 0, 0)),
                  pl.BlockSpec((None, CHUNK_A, aw), lambda c: (which(c), 0, 0))],
        out_specs=(pl.BlockSpec((CHUNK_A, aw), lambda c: (c, 0)),
                   pl.BlockSpec((CHUNK_A, aw), lambda c: (jnp.maximum(c - n_prompt_chunks, 0), 0))),
        compiler_params=_params(("arbitrary",), 40),
        name="gmlp_gate",
    )(u, vpre, ln_g.reshape(1, aw), ln_b.reshape(1, aw), ws2, bias2)


def _conv_epilogue(y, o_ref, rows, n_q_blocks, n_qk_blocks):
    c = pl.program_id(1)
    y = _silu(y)
    tc = y.shape[1]

    @pl.when(c < n_qk_blocks)
    def _():
        scale = jnp.where(c < n_q_blocks, DK ** -0.5, 1.0).astype(F32)
        for j in range(tc // DK):
            blk = y[:, j * DK:(j + 1) * DK]
            nrm = blk * lax.rsqrt(jnp.sum(blk * blk, axis=-1, keepdims=True) + EPS)
            o_ref[rows, j * DK:(j + 1) * DK] = (nrm * scale).astype(o_ref.dtype)

    @pl.when(c >= n_qk_blocks)
    def _():
        o_ref[rows, :] = y.astype(o_ref.dtype)


def _conv_prompt_kernel(x_ref, st_ref, w_ref, o_ref, *, sub, n_q_blocks, n_qk_blocks):
    t_len, tc = x_ref.shape
    w = w_ref[...]
    r8 = lax.broadcasted_iota(I32, (SUBLANES, tc), 0)

    def body(i, carry):
        r0 = pl.multiple_of(i * sub, sub)
        cur = x_ref[pl.ds(r0, sub), :]
        p0 = pl.multiple_of(jnp.maximum(r0 - SUBLANES, 0), SUBLANES)
        prev8 = jnp.where(i == 0, st_ref[...], x_ref[pl.ds(p0, SUBLANES), :])
        acc = cur * w[CONV_W - 1:CONV_W, :]
        for k in range(1, CONV_W):
            xs = pltpu.roll(cur, k, 0)
            top = jnp.where(r8 < k, pltpu.roll(prev8, k, 0), xs[:SUBLANES])
            if sub > SUBLANES:
                xs = jnp.concatenate([top, xs[SUBLANES:]], axis=0)
            else:
                xs = top
            acc = acc + xs * w[CONV_W - 1 - k:CONV_W - k, :]
        _conv_epilogue(acc, o_ref, pl.ds(r0, sub), n_q_blocks, n_qk_blocks)
        return carry

    lax.fori_loop(0, t_len // sub, body, 0)


def _conv_sample_kernel(x_ref, st_ref, w_ref, o_ref, *, seq_len, n_q_blocks, n_qk_blocks):
    rows, tc = x_ref.shape
    assert seq_len == SUBLANES
    w = w_ref[...]
    x = x_ref[...]
    st = st_ref[...]
    t = lax.broadcasted_iota(I32, (rows, tc), 0) & (seq_len - 1)
    acc = x * w[CONV_W - 1:CONV_W, :]
    for k in range(1, CONV_W):
        xs = pltpu.roll(x, k, 0)
        ss = pltpu.roll(st, rows - (seq_len - k), 0)
        acc = acc + jnp.where(t >= k, xs, ss) * w[CONV_W - 1 - k:CONV_W - k, :]
    _conv_epilogue(acc, o_ref, slice(None), n_q_blocks, n_qk_blocks)


def gdn_conv_prompt(qkv, st8, conv_w, layer, n_seq, seq_len, key_dim):
    c_dim = qkv.shape[1]
    tc = 256
    sub = _pick(seq_len, 128)
    nqb, nqkb = key_dim // tc, 2 * key_dim // tc
    return pl.pallas_call(
        functools.partial(_conv_prompt_kernel, sub=sub, n_q_blocks=nqb, n_qk_blocks=nqkb),
        out_shape=jax.ShapeDtypeStruct((n_seq * seq_len, c_dim), BF16),
        grid=(n_seq, c_dim // tc),
        in_specs=[pl.BlockSpec((seq_len, tc), lambda b, c: (b, c)),
                  pl.BlockSpec((SUBLANES, tc), lambda b, c: (b, c)),
                  pl.BlockSpec((None, CONV_W, tc), lambda b, c: (layer, 0, c))],
        out_specs=pl.BlockSpec((seq_len, tc), lambda b, c: (b, c)),
        compiler_params=_params(("arbitrary", "arbitrary"), 32),
        name="gdn_conv_prompt",
    )(qkv, st8, conv_w)


def gdn_conv_sample(qkv, row0, st8, conv_w, layer, n_rows, seq_len, key_dim):
    c_dim = qkv.shape[1]
    tc = 512
    tr = _pick(n_rows, 256)
    assert row0 % tr == 0
    rb0 = row0 // tr
    nqb, nqkb = key_dim // tc, 2 * key_dim // tc
    return pl.pallas_call(
        functools.partial(_conv_sample_kernel, seq_len=seq_len, n_q_blocks=nqb, n_qk_blocks=nqkb),
        out_shape=jax.ShapeDtypeStruct((n_rows, c_dim), BF16),
        grid=(n_rows // tr, c_dim // tc),
        in_specs=[pl.BlockSpec((tr, tc), lambda r, c: (rb0 + r, c)),
                  pl.BlockSpec((tr, tc), lambda r, c: (r, c)),
                  pl.BlockSpec((None, CONV_W, tc), lambda r, c: (layer, 0, c))],
        out_specs=pl.BlockSpec((tr, tc), lambda r, c: (r, c)),
        compiler_params=_params(("arbitrary", "arbitrary"), 32),
        name="gdn_conv_sample",
    )(qkv, st8, conv_w)


def _dot_nt(a, b):
    return lax.dot_general(a, b, (((1,), (1,)), ((), ())), preferred_element_type=F32)


def _dot_tn(a, b):
    return lax.dot_general(a, b, (((0,), (0,)), ((), ())), preferred_element_type=F32)


def _bdot(a, b):
    return jnp.dot(a.astype(BF16), b.astype(BF16), preferred_element_type=F32)


def _chunk_masks(c_len, seq_len=None):
    ri = lax.broadcasted_iota(I32, (c_len, 2 * c_len), 0)
    ci = lax.broadcasted_iota(I32, (c_len, 2 * c_len), 1)
    left = ci < c_len
    if seq_len is not None:
        shift = seq_len.bit_length() - 1
        left = left & ((ri >> shift) == (ci >> shift))
    sq = lambda m: m[:, :c_len]
    return dict(incl=sq(left & (ci <= ri)), incl_w=left & (ci <= ri), strict_w=left & (ci < ri),
                incl_t_w=left & (ri <= ci), eye_w=jnp.where(ci == ri + c_len, 1.0, 0.0),
                left_w=ci < c_len, same=sq(left) if seq_len is not None else None)


def _chunk_setup(kk_w, qkm, kf, qf, vf, gb, gt, h, n_heads, masks):
    rows = qkm.shape[0]
    lane_gb = lax.broadcasted_iota(I32, (rows, 2 * n_heads), 1)
    sub_gt = lax.broadcasted_iota(I32, (n_heads, rows), 0)
    beta = jnp.sum(jnp.where(lane_gb == h, gb, 0.0), axis=-1, keepdims=True)
    g_col = jnp.sum(jnp.where(lane_gb == h + n_heads, gb, 0.0), axis=-1, keepdims=True)
    g_row = jnp.sum(jnp.where(sub_gt == h, gt, 0.0), axis=0, keepdims=True)
    gcum = jnp.sum(jnp.where(masks["incl"], g_row, 0.0), axis=-1, keepdims=True)
    gcum_row = jnp.sum(jnp.where(masks["incl_t_w"], g_col, 0.0), axis=0, keepdims=True)
    decay_w = jnp.exp(jnp.where(masks["incl_w"], gcum - gcum_row, -jnp.inf))
    egc = jnp.exp(gcum)
    out = dict(w0=masks["eye_w"] - jnp.where(masks["strict_w"], kk_w * decay_w * beta, 0.0),
               rhs=jnp.concatenate([vf * beta, kf * (beta * egc)], axis=1),
               qk=qkm * decay_w[:, :rows], q_g=qf * egc, gcum=gcum, kf=kf)
    if masks["same"] is not None:
        out["gsum"] = jnp.sum(jnp.where(masks["same"], g_row, 0.0), axis=-1, keepdims=True)
    return out


def _product_step(w, left_w):
    wb = w.astype(BF16)
    prod = jnp.dot(wb, jnp.concatenate([wb, jnp.zeros_like(wb)], axis=0), preferred_element_type=F32)
    return jnp.where(left_w, prod, w + prod)


def _apply_inverse(w, rhs):
    rb = rhs.astype(BF16)
    return jnp.dot(w.astype(BF16), jnp.concatenate([jnp.zeros_like(rb), rb], axis=0), preferred_element_type=F32)


def _delta_kernel(q_ref, k_ref, v_ref, z_ref, gb_ref, gt_ref, nw_ref, o_init_ref, o_ref, s_ref, *,
                  n_chunks, n_heads, heads_per_qk, qk_per_step):
    c_len = CHUNK_B
    hpb = pl.program_id(1)
    hps = heads_per_qk
    n_local = qk_per_step * hps

    @pl.when(pl.program_id(2) == 0)
    def _():
        s_ref[...] = jnp.zeros_like(s_ref)

    masks = _chunk_masks(c_len)
    nw = nw_ref[...]

    def prep_stages(chunks):
        st = {}

        def setup():
            items = {}
            for c in chunks:
                rs = slice(c * c_len, (c + 1) * c_len)
                gb = gb_ref[rs, :]
                gt = gt_ref[:, rs]
                for qh in range(qk_per_step):
                    kc = k_ref[rs, qh * DK:(qh + 1) * DK]
                    qc = q_ref[rs, qh * DK:(qh + 1) * DK]
                    kk_w = _dot_nt(kc, jnp.concatenate([kc, jnp.zeros_like(kc)], axis=0))
                    qkm = _dot_nt(qc, kc)
                    kf = kc.astype(F32)
                    qf = qc.astype(F32)
                    for hh in range(hps):
                        hl = qh * hps + hh
                        vf = v_ref[rs, hl * DV:(hl + 1) * DV].astype(F32)
                        items[(c, hl)] = _chunk_setup(kk_w, qkm, kf, qf, vf, gb, gt, hpb * n_local + hl,
                                                      n_heads, masks)
            st["prep"] = items
            st["w"] = {key: p["w0"] for key, p in items.items()}

        def step():
            st["w"] = {key: _product_step(w, masks["left_w"]) for key, w in st["w"].items()}

        def solve():
            st["sols"] = {key: _apply_inverse(w, st["prep"][key]["rhs"]) for key, w in st["w"].items()}

        return st, [setup] + [step] * (c_len.bit_length() - 1) + [solve]

    state = [s_ref[0, hl] for hl in range(n_local)]

    def chain_stages(chunks, st):
        stages = []
        for c in chunks:
            tmp = {}

            def apply_state(c=c, tmp=tmp):
                tmp["ws"] = [_bdot(jnp.concatenate([st["sols"][(c, hl)][:, DV:], st["prep"][(c, hl)]["q_g"]],
                                                   axis=0), state[hl]) for hl in range(n_local)]

            def correct(c=c, tmp=tmp):
                tmp["u"] = [st["sols"][(c, hl)][:, :DV] - tmp["ws"][hl][:c_len] for hl in range(n_local)]
                tmp["out"] = [tmp["ws"][hl][c_len:] + _bdot(st["prep"][(c, hl)]["qk"], tmp["u"][hl])
                              for hl in range(n_local)]

            def advance(c=c, tmp=tmp):
                rs = slice(c * c_len, (c + 1) * c_len)
                for hl in range(n_local):
                    p = st["prep"][(c, hl)]
                    g_last = p["gcum"][c_len - 1:c_len, :]
                    k_tail = p["kf"] * jnp.exp(g_last - p["gcum"])
                    state[hl] = (state[hl] * jnp.exp(g_last)
                                 + _dot_tn(k_tail.astype(BF16), tmp["u"][hl].astype(BF16)))
                for hl in range(n_local):
                    hs = slice(hl * DV, (hl + 1) * DV)
                    zf = z_ref[rs, hs].astype(F32)
                    o_ref[rs, hs] = (_rms(tmp["out"][hl], nw) * _silu(zf)).astype(o_ref.dtype)

            stages += [apply_state, correct, advance]
        return stages

    half = max(n_chunks // 2, 1)
    st_a, prep_a = prep_stages(range(half))
    st_b, prep_b = prep_stages(range(half, n_chunks))
    for stage in prep_a:
        stage()
    chain_a = chain_stages(range(half), st_a)
    fill = iter(prep_b if half < n_chunks else [])
    per = -(-len(prep_b) // len(chain_a))
    for stage in chain_a:
        stage()
        for _ in range(per):
            nxt = next(fill, None)
            if nxt is not None:
                nxt()
    for nxt in fill:
        nxt()
    if half < n_chunks:
        for stage in chain_stages(range(half, n_chunks), st_b):
            stage()
    for hl in range(n_local):
        s_ref[0, hl] = state[hl]


def gated_delta_prompt(qkvc, z, gb, gt, norm_w, n_seq, seq_len, key_dim, n_heads):
    n_qk = key_dim // DK
    hps = n_heads // n_qk
    qps = 4 if n_qk % 4 == 0 else 1
    tc = _pick(seq_len, 256)
    nt = seq_len // tc
    qw, vw = qps * DK, qps * hps * DV
    k_blk0, v_blk0 = key_dim // qw, 2 * key_dim // vw
    rb = lambda b, t: b * nt + t
    return pl.pallas_call(
        functools.partial(_delta_kernel, n_chunks=tc // CHUNK_B, n_heads=n_heads, heads_per_qk=hps,
                          qk_per_step=qps),
        out_shape=(jax.ShapeDtypeStruct((z.shape[0], n_heads * DV), BF16),
                   jax.ShapeDtypeStruct((n_seq, n_heads, DK, DV), F32)),
        grid=(n_seq, n_qk // qps, nt),
        in_specs=[pl.BlockSpec((tc, qw), lambda b, h, t: (rb(b, t), h)),
                  pl.BlockSpec((tc, qw), lambda b, h, t: (rb(b, t), k_blk0 + h)),
                  pl.BlockSpec((tc, vw), lambda b, h, t: (rb(b, t), v_blk0 + h)),
                  pl.BlockSpec((tc, vw), lambda b, h, t: (rb(b, t), h)),
                  pl.BlockSpec((tc, 2 * n_heads), lambda b, h, t: (rb(b, t), 0)),
                  pl.BlockSpec((n_heads, tc), lambda b, h, t: (0, rb(b, t))),
                  pl.BlockSpec((1, DV), lambda b, h, t: (0, 0)),
                  pl.BlockSpec(memory_space=pl.ANY)],
        out_specs=(pl.BlockSpec((tc, vw), lambda b, h, t: (rb(b, t), h)),
                   pl.BlockSpec((1, qps * hps, DK, DV), lambda b, h, t: (b, h, 0, 0))),
        input_output_aliases={7: 0},
        compiler_params=_params(("arbitrary", "arbitrary", "arbitrary"), 32),
        name="gated_delta_prompt",
    )(qkvc, qkvc, qkvc, z, gb, gt, norm_w.reshape(1, DV), jnp.zeros((z.shape[0], n_heads * DV), BF16))


def _delta_sample_kernel(q_ref, k_ref, v_ref, z_ref, gb_ref, gt_ref, nw_ref, s0_ref, *rest,
                         seq_len, n_heads, heads_per_qk, slab):
    o_ref, s_ref = rest[-2:]
    if slab is not None:
        for other in range(s_ref.shape[0]):
            if other != slab:
                s_ref[other] = jnp.zeros(s_ref.shape[1:], F32)
        s_ref = s_ref.at[slab]
    c_len = CHUNK_B
    hpb = pl.program_id(1)
    hps = heads_per_qk
    rows = q_ref.shape[0]
    per = c_len // seq_len
    masks = _chunk_masks(c_len, seq_len)
    nw = nw_ref[...]

    prep = []
    for c in range(rows // c_len):
        rs = slice(c * c_len, (c + 1) * c_len)
        kc = k_ref[rs, :]
        qc = q_ref[rs, :]
        kk_w = _dot_nt(kc, jnp.concatenate([kc, jnp.zeros_like(kc)], axis=0))
        qkm = _dot_nt(qc, kc)
        kf = kc.astype(F32)
        qf = qc.astype(F32)
        gb = gb_ref[rs, :]
        gt = gt_ref[:, rs]
        for hh in range(hps):
            vf = v_ref[rs, hh * DV:(hh + 1) * DV].astype(F32)
            prep.append(_chunk_setup(kk_w, qkm, kf, qf, vf, gb, gt, hpb * hps + hh, n_heads, masks))
    ws_ = [p["w0"] for p in prep]
    for _ in range(seq_len.bit_length() - 1):
        ws_ = [_product_step(w, masks["left_w"]) for w in ws_]
    sols = [_apply_inverse(w, p["rhs"]) for w, p in zip(ws_, prep)]

    for c in range(rows // c_len):
        rs = slice(c * c_len, (c + 1) * c_len)
        for hh in range(hps):
            p, sol = prep[c * hps + hh], sols[c * hps + hh]
            u_v, w_k = sol[:, :DV], sol[:, DV:]
            k_tail = p["kf"] * jnp.exp(p["gsum"] - p["gcum"])
            g_tail = jnp.exp(p["gsum"])
            s_old = [s0_ref[c * per + i, hh] for i in range(per)]
            seq = [slice(i * seq_len, (i + 1) * seq_len) for i in range(per)]
            ws = [_bdot(jnp.concatenate([w_k[r], p["q_g"][r]], axis=0), s) for r, s in zip(seq, s_old)]
            u_new = jnp.concatenate([u_v[r] - w[:seq_len] for r, w in zip(seq, ws)], axis=0)
            out = jnp.concatenate([w[seq_len:] for w in ws], axis=0) + _bdot(p["qk"], u_new)
            for i, r in enumerate(seq):
                s_ref[c * per + i, hh] = (s_old[i] * g_tail[i * seq_len:i * seq_len + 1, :]
                                          + _dot_tn(k_tail[r].astype(BF16), u_new[r].astype(BF16)))
            hs = slice(hh * DV, (hh + 1) * DV)
            zf = z_ref[rs, hs].astype(F32)
            o_ref[rs, hs] = (_rms(out, nw) * _silu(zf)).astype(o_ref.dtype)


def gated_delta_sample(qkvc, z, gb, gt, norm_w, s0, layer, row0, n_seq, seq_len, key_dim, o_all, s_all):
    n_heads = s0.shape[2]
    n_qk = key_dim // DK
    hps = n_heads // n_qk
    tr = LANES
    sps = tr // seq_len
    assert row0 % tr == 0 and n_seq % sps == 0 and seq_len == SUBLANES
    rb0 = row0 // tr
    vw = hps * DV
    v_blk0 = 2 * key_dim // vw
    args = [qkvc, qkvc, qkvc, z, gb, gt, norm_w.reshape(1, DV), s0, o_all]
    in_specs = [pl.BlockSpec((tr, DK), lambda g, h: (g, h)),
                pl.BlockSpec((tr, DK), lambda g, h: (g, n_qk + h)),
                pl.BlockSpec((tr, vw), lambda g, h: (g, v_blk0 + h)),
                pl.BlockSpec((tr, vw), lambda g, h: (rb0 + g, h)),
                pl.BlockSpec((tr, 2 * n_heads), lambda g, h: (rb0 + g, 0)),
                pl.BlockSpec((n_heads, tr), lambda g, h: (0, rb0 + g)),
                pl.BlockSpec((1, DV), lambda g, h: (0, 0)),
                pl.BlockSpec((None, sps, hps, DK, DV), lambda g, h: (layer, g, h, 0, 0)),
                pl.BlockSpec(memory_space=pl.ANY)]
    aliases = {8: 0}
    if s_all is not None:
        args.append(s_all)
        in_specs.append(pl.BlockSpec(memory_space=pl.ANY))
        aliases[9] = 1
        slab = None
        s_spec = pl.BlockSpec((None, sps, hps, DK, DV), lambda g, h: (layer, g, h, 0, 0))
    else:
        slab = layer
        s_spec = pl.BlockSpec((s0.shape[0], sps, hps, DK, DV), lambda g, h: (0, g, h, 0, 0))
    return pl.pallas_call(
        functools.partial(_delta_sample_kernel, seq_len=seq_len, n_heads=n_heads, heads_per_qk=hps, slab=slab),
        out_shape=(jax.ShapeDtypeStruct(o_all.shape, BF16), jax.ShapeDtypeStruct(s0.shape, F32)),
        grid=(n_seq // sps, n_qk),
        in_specs=in_specs,
        out_specs=(pl.BlockSpec((tr, vw), lambda g, h: (rb0 + g, h)), s_spec),
        input_output_aliases=aliases,
        compiler_params=_params(("arbitrary", "arbitrary"), 32),
        name="gated_delta_sample",
    )(*args)


def _router_kernel(x_ref, nw_ref, wr_ref, br_ref, h_ref, sel_ref, gate_ref, cnt_ref, cnt_sc):
    tm = x_ref.shape[0]

    @pl.when(pl.program_id(0) == 0)
    def _():
        cnt_sc[...] = jnp.zeros_like(cnt_sc)

    h = _rms(x_ref[...], nw_ref[...])
    h_ref[...] = h
    h_hi = h.astype(BF16)
    h_lo = (h - h_hi.astype(F32)).astype(BF16)
    h3 = jnp.concatenate([h_hi, h_hi, h_lo], axis=1)
    logits = jnp.dot(h3, wr_ref[...], preferred_element_type=F32) + br_ref[...]
    lane = lax.broadcasted_iota(I32, (tm, LANES), 1)
    lane_f = lane.astype(F32)
    big = float(LANES)

    is_g = lane < N_GROUPS
    gl = jnp.where(is_g, logits, -jnp.inf)
    ge = jnp.exp(gl - jnp.max(gl, axis=-1, keepdims=True))
    gp = ge / jnp.sum(ge, axis=-1, keepdims=True)
    p_group = jnp.max(gp, axis=-1, keepdims=True)
    g_sel = jnp.min(jnp.where(is_g & (gp == p_group), lane_f, big), axis=-1, keepdims=True)

    e_lane = lane - N_GROUPS
    in_grp = (e_lane >= 0) & (e_lane < N_EXPERTS) & ((e_lane >> 3).astype(F32) == g_sel)
    el = jnp.where(in_grp, logits, -jnp.inf)
    ee = jnp.exp(el - jnp.max(el, axis=-1, keepdims=True))
    ep = ee / jnp.sum(ee, axis=-1, keepdims=True)
    p1 = jnp.max(ep, axis=-1, keepdims=True)
    i1 = jnp.min(jnp.where(in_grp & (ep == p1), lane_f, big), axis=-1, keepdims=True)
    rest = in_grp & (lane_f != i1)
    p2 = jnp.max(jnp.where(rest, ep, -1.0), axis=-1, keepdims=True)
    i2 = jnp.min(jnp.where(rest & (ep == p2), lane_f, big), axis=-1, keepdims=True)
    denom = p1 + p2
    gate1 = p_group * p1 / denom
    gate2 = p_group * p2 / denom
    e1 = i1 - float(N_GROUPS)
    e2 = i2 - float(N_GROUPS)

    onehot = (lane_f == e1) | (lane_f == e2)
    oh = jnp.where(onehot, 1.0, 0.0)
    rr = lax.broadcasted_iota(I32, (tm, tm), 0)
    cc = lax.broadcasted_iota(I32, (tm, tm), 1)
    tri = jnp.where(cc < rr, 1.0, 0.0).astype(BF16)
    before = jnp.dot(tri, oh.astype(BF16), preferred_element_type=F32) + cnt_sc[...]
    r1 = jnp.sum(jnp.where(lane_f == e1, before, 0.0), axis=-1, keepdims=True)
    r2 = jnp.sum(jnp.where(lane_f == e2, before, 0.0), axis=-1, keepdims=True)
    cnt_sc[...] = cnt_sc[...] + jnp.sum(oh, axis=0, keepdims=True)
    cnt_ref[...] = cnt_sc[...]

    sel = jnp.where(lane == 0, e1, jnp.where(lane == 1, e2, jnp.where(lane == 2, r1, jnp.where(lane == 3, r2, 0.0))))
    sel_ref[...] = sel.astype(I32)
    gate_ref[...] = jnp.where(lane == 0, gate1, jnp.where(lane == 1, gate2, 0.0))


def moe_router(x, norm_w, w_route, b_route):
    n, d = x.shape
    tm = _pick(n, 256)
    return pl.pallas_call(
        _router_kernel,
        out_shape=(jax.ShapeDtypeStruct((n, d), F32),
                   jax.ShapeDtypeStruct((n, LANES), I32),
                   jax.ShapeDtypeStruct((n, LANES), F32),
                   jax.ShapeDtypeStruct((1, LANES), F32)),
        grid=(n // tm,),
        in_specs=[pl.BlockSpec((tm, d), lambda i: (i, 0)),
                  pl.BlockSpec((1, d), lambda i: (0, 0)),
                  pl.BlockSpec((3 * d, LANES), lambda i: (0, 0)),
                  pl.BlockSpec((1, LANES), lambda i: (0, 0))],
        out_specs=(pl.BlockSpec((tm, d), lambda i: (i, 0)),
                   pl.BlockSpec((tm, LANES), lambda i: (i, 0)),
                   pl.BlockSpec((tm, LANES), lambda i: (i, 0)),
                   pl.BlockSpec((1, LANES), lambda i: (0, 0))),
        scratch_shapes=[pltpu.VMEM((1, LANES), F32)],
        compiler_params=_params(("arbitrary",), 32),
        name="moe_router",
    )(x, norm_w.reshape(1, d), w_route, b_route)


ISSUE_UNROLL = 8


def _slot(slot_ref, r, k):
    return slot_ref[0, 0, TOP_K * r + k]


def _row_copy_out(h_ref, xs_ref, sem, r, d):
    return pltpu.make_async_copy(h_ref.at[pl.ds(r, 1), :], xs_ref.at[pl.ds(d, 1), :], sem)


def _dispatch_kernel(slot_ref, h_ref, xs_in_ref, xs_ref, sem):
    del xs_in_ref
    tm = h_ref.shape[0]

    def issue(r, carry):
        for k in range(TOP_K):
            _row_copy_out(h_ref, xs_ref, sem, r, _slot(slot_ref, r, k)).start()
        return carry

    lax.fori_loop(0, tm, issue, 0, unroll=ISSUE_UNROLL)
    for k in range(TOP_K):
        pltpu.make_async_copy(h_ref, xs_ref.at[pl.ds(0, tm), :], sem).wait()


def moe_dispatch(h, slot3, n_slots):
    n, d = h.shape
    tm = slot3.shape[2] // TOP_K
    xs0 = jnp.zeros((n_slots, d), F32)
    return pl.pallas_call(
        _dispatch_kernel,
        out_shape=jax.ShapeDtypeStruct((n_slots, d), F32),
        grid=(n // tm,),
        in_specs=[pl.BlockSpec((1, 1, TOP_K * tm), lambda i: (i, 0, 0), memory_space=pltpu.SMEM),
                  pl.BlockSpec((tm, d), lambda i: (i, 0)),
                  pl.BlockSpec(memory_space=pl.ANY)],
        out_specs=pl.BlockSpec(memory_space=pl.ANY),
        scratch_shapes=[pltpu.SemaphoreType.DMA(())],
        input_output_aliases={2: 0},
        compiler_params=_params(("arbitrary",), 32),
        name="moe_dispatch",
    )(slot3, h, xs0)


WEIGHT_SLOTS = 2


def _ffn_kernel(bseq_ref, first_ref, sexp_ref, meta_ref, x_ref, wg_hbm, wu_hbm, wd_hbm, o_ref,
                wgf, wuf, wdf, wgb, wub, wdb, sem, *, layer):
    b = pl.program_id(0)
    n_used, n_seq = meta_ref[0], meta_ref[1]
    valid = b < n_used
    j = bseq_ref[b]
    slot = lax.rem(j, WEIGHT_SLOTS)

    def fetch(jj, sl):
        e = sexp_ref[jj]
        return [pltpu.make_async_copy(hbm.at[layer, e], buf.at[sl], sem.at[sl, i])
                for i, (hbm, buf) in enumerate(((wg_hbm, wgf), (wu_hbm, wuf), (wd_hbm, wdf)))]

    for ahead in range(WEIGHT_SLOTS):
        @pl.when((b == 0) & (ahead < n_seq))
        def _(ahead=ahead):
            for cp in fetch(ahead, ahead):
                cp.start()

    @pl.when(valid & (first_ref[b] == 1))
    def _():
        for cp in fetch(j, slot):
            cp.wait()
        for s in range(WEIGHT_SLOTS):
            @pl.when(slot == s)
            def _(s=s):
                wgb[...] = wgf[s].astype(BF16)
                wub[...] = wuf[s].astype(BF16)
                wdb[...] = wdf[s].astype(BF16)

        @pl.when(j + WEIGHT_SLOTS < n_seq)
        def _():
            for cp in fetch(j + WEIGHT_SLOTS, slot):
                cp.start()

    @pl.when(valid)
    def _():
        x = x_ref[...].astype(BF16)
        gate = jnp.dot(x, wgb[...], preferred_element_type=F32)
        up = jnp.dot(x, wub[...], preferred_element_type=F32)
        hid = (_silu(gate) * up).astype(BF16)
        o_ref[...] = jnp.dot(hid, wdb[...], preferred_element_type=F32)

    @pl.when(jnp.logical_not(valid))
    def _():
        o_ref[...] = jnp.zeros_like(o_ref)


def moe_experts(xs, block_seq, block_first, seq_expert, meta, w_gate, w_up, w_down, layer):
    n_slots, d = xs.shape
    de = w_gate.shape[-1]
    n_blocks = n_slots // MOE_ROWS
    xmap = lambda b, bs, fi, se, me: (jnp.minimum(b, jnp.maximum(me[0] - 1, 0)), 0)
    grid_spec = pltpu.PrefetchScalarGridSpec(
        num_scalar_prefetch=4,
        grid=(n_blocks,),
        in_specs=[pl.BlockSpec((MOE_ROWS, d), xmap),
                  pl.BlockSpec(memory_space=pl.ANY),
                  pl.BlockSpec(memory_space=pl.ANY),
                  pl.BlockSpec(memory_space=pl.ANY)],
        out_specs=pl.BlockSpec((MOE_ROWS, d), lambda b, bs, fi, se, me: (b, 0)),
        scratch_shapes=[pltpu.VMEM((WEIGHT_SLOTS, d, de), F32), pltpu.VMEM((WEIGHT_SLOTS, d, de), F32),
                        pltpu.VMEM((WEIGHT_SLOTS, de, d), F32),
                        pltpu.VMEM((d, de), BF16), pltpu.VMEM((d, de), BF16), pltpu.VMEM((de, d), BF16),
                        pltpu.SemaphoreType.DMA((WEIGHT_SLOTS, 3))],
    )
    return pl.pallas_call(
        functools.partial(_ffn_kernel, layer=layer),
        out_shape=jax.ShapeDtypeStruct((n_slots, d), F32),
        grid_spec=grid_spec,
        compiler_params=_params(("arbitrary",), 56),
        name="moe_experts",
    )(block_seq, block_first, seq_expert, meta, xs, w_gate, w_up, w_down)


def _row_copy_in(yb_ref, buf_ref, sem, k, r, d):
    return pltpu.make_async_copy(yb_ref.at[pl.ds(d, 1), :], buf_ref.at[k, pl.ds(r, 1), :], sem)


def _combine_kernel(slot_ref, x_ref, gate_ref, nw_ref, yb_ref, o1_ref, o2_ref, buf_ref, sem, *, split_step):
    tm = x_ref.shape[0]

    def issue(r, carry):
        for k in range(TOP_K):
            _row_copy_in(yb_ref, buf_ref, sem, k, r, _slot(slot_ref, r, k)).start()
        return carry

    lax.fori_loop(0, tm, issue, 0, unroll=ISSUE_UNROLL)
    for k in range(TOP_K):
        pltpu.make_async_copy(yb_ref.at[pl.ds(0, tm), :], buf_ref.at[k], sem).wait()

    g = gate_ref[...]
    y = buf_ref[0] * g[:, 0:1] + buf_ref[1] * g[:, 1:2]
    xn = x_ref[...] + y
    hn = _rms(xn, nw_ref[...])
    if split_step is None:
        o1_ref[...] = xn
        o2_ref[...] = hn.astype(o2_ref.dtype)
    else:
        @pl.when(pl.program_id(0) < split_step)
        def _():
            o1_ref[...] = hn

        @pl.when(pl.program_id(0) >= split_step)
        def _():
            o2_ref[...] = hn


def moe_combine(x, yb, slot3, gates, next_norm_w, split_rows=None):
    n, d = x.shape
    tm = slot3.shape[2] // TOP_K
    if split_rows is None:
        split_step = None
        out_shape = (jax.ShapeDtypeStruct((n, d), F32), jax.ShapeDtypeStruct((n, d), BF16))
        out_specs = (pl.BlockSpec((tm, d), lambda i: (i, 0)), pl.BlockSpec((tm, d), lambda i: (i, 0)))
    else:
        assert split_rows % tm == 0 and 0 < split_rows < n
        split_step = split_rows // tm
        out_shape = (jax.ShapeDtypeStruct((split_rows, d), F32), jax.ShapeDtypeStruct((n - split_rows, d), F32))
        out_specs = (pl.BlockSpec((tm, d), lambda i: (jnp.minimum(i, split_step - 1), 0)),
                     pl.BlockSpec((tm, d), lambda i: (jnp.maximum(i - split_step, 0), 0)))
    return pl.pallas_call(
        functools.partial(_combine_kernel, split_step=split_step),
        out_shape=out_shape,
        grid=(n // tm,),
        in_specs=[pl.BlockSpec((1, 1, TOP_K * tm), lambda i: (i, 0, 0), memory_space=pltpu.SMEM),
                  pl.BlockSpec((tm, d), lambda i: (i, 0)),
                  pl.BlockSpec((tm, LANES), lambda i: (i, 0)),
                  pl.BlockSpec((1, d), lambda i: (0, 0)),
                  pl.BlockSpec(memory_space=pl.ANY)],
        out_specs=out_specs,
        scratch_shapes=[pltpu.VMEM((TOP_K, tm, d), F32), pltpu.SemaphoreType.DMA(())],
        compiler_params=_params(("arbitrary",), 40),
        name="moe_combine",
    )(slot3, x, gates, next_norm_w.reshape(1, d), yb)


def hier_moe(x, ffn_norm_w, w_route, b_route, w_gate, w_up, w_down, layer, next_norm_w, split_rows=None):
    n, _ = x.shape
    h, sel, gates, cnt = moe_router(x, ffn_norm_w, w_route, b_route)
    counts = cnt[0, :N_EXPERTS].astype(I32)
    padded = (counts + MOE_ROWS - 1) // MOE_ROWS * MOE_ROWS
    pend = jnp.cumsum(padded)
    pstart = (pend - padded).astype(I32)
    tm = _pick(n, 256)
    expert, rank = sel[:, :TOP_K], sel[:, TOP_K:2 * TOP_K]
    is_e = expert[:, :, None] == jnp.arange(N_EXPERTS, dtype=I32)
    slot3 = (rank + jnp.sum(jnp.where(is_e, pstart, 0), axis=-1)).reshape(n // tm, 1, TOP_K * tm)
    n_blocks = -(-(n * TOP_K) // MOE_ROWS) + N_EXPERTS
    n_used = pend[-1] // MOE_ROWS
    blk = jnp.arange(n_blocks, dtype=I32)
    last = jnp.maximum(n_used - 1, 0)
    bstart = jnp.minimum(blk, last) * MOE_ROWS
    block_expert = jnp.minimum(jnp.sum(pend[None, :] <= bstart[:, None], axis=1), N_EXPERTS - 1).astype(I32)
    block_first = (bstart == pstart[block_expert]).astype(I32)
    used = counts > 0
    pos = jnp.cumsum(used.astype(I32)) - 1
    ids = jnp.arange(N_EXPERTS, dtype=I32)
    seq_expert = jnp.sum(jnp.where(used[None, :] & (pos[None, :] == ids[:, None]), ids[None, :], 0), axis=1)
    meta = jnp.stack([n_used, jnp.sum(used.astype(I32))]).astype(I32)
    xs = moe_dispatch(h, slot3, n_blocks * MOE_ROWS)
    yb = moe_experts(xs, pos[block_expert], block_first, seq_expert.astype(I32), meta, w_gate, w_up, w_down, layer)
    return moe_combine(x, yb, slot3, gates, next_norm_w, split_rows)


def kernel(x_prompt, x_sample, state_delta_S, state_delta_conv, norm_mixer_w, norm_ffn_w, norm_final_w,
           a_w_in, a_ln_g, a_ln_b, a_w_s, a_b_s, a_w_out,
           b_w_in, b_conv_w, b_a_log, b_dt_bias, b_norm_w, b_w_out,
           moe_w_group, moe_b_group, moe_w_router, moe_b_router, moe_w_gate, moe_w_up, moe_w_down):
    nb, seq, d = x_prompt.shape
    db, dseq, _ = x_sample.shape
    depth = norm_mixer_w.shape[0]
    n_p, n_s = nb * seq, db * dseq
    n = n_p + n_s
    a_width = a_ln_g.shape[1]
    n_groups_a = a_width // A_GROUP_DIM
    n_heads = b_a_log.shape[1]
    value_dim = n_heads * DV
    conv_dim = b_conv_w.shape[2]
    key_dim = (conv_dim - value_dim) // 2
    assert seq % CHUNK_A == 0 and n_s % CHUNK_A == 0 and CHUNK_A % dseq == 0 and dseq == SUBLANES
    assert seq % CHUNK_B == 0 and dseq <= CHUNK_B and dseq >= CONV_W - 1


    causal = jnp.tril(jnp.ones((CHUNK_A, CHUNK_A), bool))
    ws_p = jnp.where(causal, a_w_s, 0.0)
    per = CHUNK_A // dseq
    ws_s = jnp.where(causal[:dseq, :dseq], a_w_s[:, :, :dseq, :dseq], 0.0)
    ws_s = jnp.einsum("ij,lgts->lgitjs", jnp.eye(per, dtype=F32), ws_s).reshape(a_w_s.shape)
    ws2 = jnp.stack([ws_p, ws_s], axis=1).astype(BF16)
    bias_p = jnp.repeat(jnp.swapaxes(a_b_s, 1, 2), A_GROUP_DIM, axis=2)
    bias_s = jnp.tile(bias_p[:, :dseq], (1, per, 1))
    bias2 = jnp.stack([bias_p, bias_s], axis=1)

    w_route = jnp.concatenate([moe_w_group, moe_w_router,
                               jnp.zeros((depth, d, LANES - N_GROUPS - N_EXPERTS), F32)], axis=2)
    w_route_hi = w_route.astype(BF16)
    w_route_lo = (w_route - w_route_hi.astype(F32)).astype(BF16)
    w_route = jnp.concatenate([w_route_hi, w_route_lo, w_route_hi], axis=1)
    b_route = jnp.concatenate([moe_b_group, moe_b_router,
                               jnp.zeros((depth, LANES - N_GROUPS - N_EXPERTS), F32)], axis=1)
    b_w_in_t = jnp.swapaxes(b_w_in, 1, 2)
    zeros_h = jnp.zeros_like(b_a_log)
    a_log2 = jnp.concatenate([zeros_h, b_a_log], axis=1)
    dt2 = jnp.concatenate([zeros_h, b_dt_bias], axis=1)

    chunk_v, s_prompt, conv_prompt, conv_sample = [], [], [], []
    s_sample = None
    x, h = concat_rmsnorm(x_prompt.reshape(n_p, d), x_sample.reshape(n_s, d), norm_mixer_w[0])
    for i in range(depth):
        j = i // 2
        if i % 2 == 0:
            u = matmul(h, a_w_in, j, 0, a_width, BF16, act="gelu")
            vpre = matmul(h, a_w_in, j, a_width, a_width, F32, act="gelu")
            gated, v_s = gmlp_gate(u, vpre, a_ln_g[j], a_ln_b[j], ws2[j], bias2[j], n_p // CHUNK_A)
            chunk_v.append(v_s.reshape(db, dseq, a_width))
            x = matmul(gated, a_w_out, j, 0, d, F32, res=x)
        else:
            qkv = matmul(h, b_w_in_t, j, 0, conv_dim, F32, w_t=True)
            z = matmul(h, b_w_in_t, j, conv_dim, value_dim, BF16, w_t=True)
            gb = matmul(h, b_w_in_t, j, conv_dim + value_dim, 2 * n_heads, F32, act="gdn_gates",
                        extra=(a_log2[j:j + 1], dt2[j:j + 1]), w_t=True)
            gt = gb[:, n_heads:].T
            keep = CONV_W - 1
            conv_prompt.append(jnp.stack([qkv[(b + 1) * seq - keep:(b + 1) * seq] for b in range(nb)]))
            conv_sample.append(qkv[n_p:].reshape(db, dseq, conv_dim)[:, dseq - keep:])
            pad8 = ((0, 0), (SUBLANES - keep, 0), (0, 0))
            st8_p = jnp.zeros((nb * SUBLANES, conv_dim), F32)
            st8_s = jnp.pad(state_delta_conv[j], pad8).reshape(db * SUBLANES, conv_dim)
            qc_p = gdn_conv_prompt(qkv, st8_p, b_conv_w, j, nb, seq, key_dim)
            qc_s = gdn_conv_sample(qkv, n_p, st8_s, b_conv_w, j, n_s, dseq, key_dim)
            o_p, s_p = gated_delta_prompt(qc_p, z, gb, gt, b_norm_w[j], nb, seq, key_dim, n_heads)
            gated, s_sample = gated_delta_sample(qc_s, z, gb, gt, b_norm_w[j], state_delta_S, j, n_p, db, dseq,
                                                 key_dim, o_p, s_sample)
            s_prompt.append(s_p)
            x = matmul(gated, b_w_out, j, 0, d, F32, res=x)
        moe_w = (norm_ffn_w[i], w_route[i], b_route[i:i + 1], moe_w_gate, moe_w_up, moe_w_down, i)
        if i < depth - 1:
            x, h = hier_moe(x, *moe_w, norm_mixer_w[i + 1])
        else:
            y_p, y_s = hier_moe(x, *moe_w, norm_final_w, split_rows=n_p)

    y_prompt = y_p.reshape(nb, seq, d)
    y_sample = y_s.reshape(db, dseq, d)
    return (y_prompt, y_sample, jnp.stack(chunk_v), jnp.stack(s_prompt), jnp.stack(conv_prompt),
            s_sample, jnp.stack(conv_sample))
```
